```python
import math
import jax, jax.numpy as jnp
from jax import lax
import numpy as np

D_MODEL = 1024
BATCH = 16
SEQ = 4096
DEPTH = 2

HEAD_DIM = 64
ROT_DIM = HEAD_DIM // 4
ROPE_THETA = 500000.0
N_EVEN = (DEPTH + 1) // 2
N_ODD = DEPTH // 2
BLK = 128
A_HEADS = 8
A_PATTERNS = ((128, 1), (512, 4), (2048, 16))
A_WIDTH = A_HEADS * HEAD_DIM
B_HEADS = 4
B_QK_WIDTH = B_HEADS * 2 * HEAD_DIM
B_WIDTH = B_HEADS * 2 * HEAD_DIM
QKV_COLS = 3 * A_WIDTH + 2 * B_QK_WIDTH + B_WIDTH
MIX_WIDTH = A_WIDTH + B_WIDTH
RWKV_HEADS = D_MODEL // HEAD_DIM
DECAY_LORA = 64
ICLR_LORA = 64
GATE_LORA = 160
GN_EPS = 64e-5
D_FF = 4 * D_MODEL
EPS = 1e-5

kernel_name = "hybrid_dilated_diff_rwkv7_block"


def rmsnorm(x, g):
    xf = x.astype(jnp.float32)
    y = xf * lax.rsqrt(jnp.mean(xf * xf, axis=-1, keepdims=True) + EPS)
    return (y * g.astype(jnp.float32)).astype(x.dtype)


def partial_rope(t, pos):
    half = ROT_DIM // 2
    inv_freq = ROPE_THETA ** (-jnp.arange(half, dtype=jnp.float32) / half)
    ang = pos.astype(jnp.float32)[:, None] * inv_freq[None, :]
    cos, sin = jnp.cos(ang), jnp.sin(ang)
    tf = t.astype(jnp.float32)
    t1, t2 = tf[..., :half], tf[..., half:ROT_DIM]
    out = jnp.concatenate([t1 * cos - t2 * sin, t2 * cos + t1 * sin, tf[..., ROT_DIM:]], axis=-1)
    return out.astype(t.dtype)


def dilated_window_attention(q, k, v, window, dilation):
    B, H, S, Dh = q.shape
    w_sub = window // dilation
    L = S // dilation
    nb = -(-L // BLK)
    Lp = nb * BLK

    def strided(t):
        t = t.reshape(B, H, L, dilation, Dh).transpose(0, 1, 3, 2, 4)
        return jnp.pad(t, ((0, 0), (0, 0), (0, 0), (0, Lp - L), (0, 0)))

    def banded(t):
        tb = jnp.pad(strided(t), ((0, 0), (0, 0), (0, 0), (BLK, 0), (0, 0)))
        tb = tb.reshape(B, H, dilation, nb + 1, BLK, Dh)
        return jnp.concatenate([tb[:, :, :, :-1], tb[:, :, :, 1:]], axis=4)

    qs = strided(q).reshape(B, H, dilation, nb, BLK, Dh)
    kw, vw = banded(k), banded(v)
    s = jnp.einsum('bhrnqd,bhrnkd->bhrnqk', qs, kw).astype(jnp.float32)
    qi = jnp.arange(BLK)[None, :, None]
    kc = jnp.arange(2 * BLK)[None, None, :]
    blk = jnp.arange(nb)[:, None, None]
    dist = BLK + qi - kc
    mask = (dist >= 0) & (dist <= w_sub) & ((blk - 1) * BLK + kc >= 0)
    s = jnp.where(mask, s, -jnp.inf)
    lse = jax.nn.logsumexp(s, axis=-1)
    p = jnp.exp(s - lse[..., None])
    o = jnp.einsum('bhrnqk,bhrnkd->bhrnqd', p.astype(v.dtype), vw)
    o = o.reshape(B, H, dilation, Lp, Dh)[:, :, :, :L].transpose(0, 1, 3, 2, 4).reshape(B, H, S, Dh)
    lse = lse.reshape(B, H, dilation, Lp)[..., :L].transpose(0, 1, 3, 2).reshape(B, H, S)
    return o, lse


def mixture_of_dilations(q, k, v):
    outs, lses = [], []
    for window, dilation in A_PATTERNS:
        o, l = dilated_window_attention(q, k, v, window, dilation)
        outs.append(o.astype(jnp.float32))
        lses.append(l)
    wts = jax.nn.softmax(jnp.stack(lses), axis=0)
    return jnp.sum(wts[..., None] * jnp.stack(outs), axis=0).astype(q.dtype)


def differential_attention(q, k, v, lam, gain, lam_init):
    B, H, _, S, Dh = q.shape
    nq = S // BLK
    qb = jnp.moveaxis(q.reshape(B, H, 2, nq, BLK, Dh), 3, 0)
    kpos = jnp.arange(S)

    def block(args):
        qblk, n = args
        s = jnp.einsum('bhcqd,bhckd->bhcqk', qblk, k).astype(jnp.float32)
        qpos = n * BLK + jnp.arange(BLK)
        s = jnp.where(qpos[:, None] >= kpos[None, :], s, -jnp.inf)
        p = jax.nn.softmax(s, axis=-1)
        attn = p[:, :, 0] - lam * p[:, :, 1]
        return jnp.einsum('bhqk,bhkd->bhqd', attn.astype(v.dtype), v)

    o = lax.map(block, (qb, jnp.arange(nq)))
    o = jnp.moveaxis(o, 0, 2).reshape(B, H, S, 2 * Dh)
    return rmsnorm(o, gain) * (1.0 - lam_init)


def hybrid_attention(h, w_in, w_out, lam_p, subln, lam_init):
    B, S, _ = h.shape
    pos = jnp.arange(S)
    scale = HEAD_DIM ** -0.5
    proj = h @ w_in
    cuts = [A_WIDTH, 2 * A_WIDTH, 3 * A_WIDTH, 3 * A_WIDTH + B_QK_WIDTH, 3 * A_WIDTH + 2 * B_QK_WIDTH]
    aq, ak, av, bq, bk, bv = jnp.split(proj, cuts, axis=-1)

    def heads(t, n):
        return t.reshape(B, S, n, -1).transpose(0, 2, 1, 3)

    aq = partial_rope(heads(aq, A_HEADS), pos) * scale
    ak = partial_rope(heads(ak, A_HEADS), pos)
    oa = mixture_of_dilations(aq, ak, heads(av, A_HEADS))
    oa = oa.transpose(0, 2, 1, 3).reshape(B, S, A_WIDTH)

    def two_maps(t):
        return t.reshape(B, S, B_HEADS, 2, HEAD_DIM).transpose(0, 2, 3, 1, 4)

    bq = partial_rope(two_maps(bq), pos) * scale
    bk = partial_rope(two_maps(bk), pos)
    lp = lam_p.astype(jnp.float32)
    lam = jnp.exp(jnp.sum(lp[0] * lp[1])) - jnp.exp(jnp.sum(lp[2] * lp[3])) + lam_init
    ob = differential_attention(bq, bk, heads(bv, B_HEADS), lam, subln, lam_init)
    ob = ob.transpose(0, 2, 1, 3).reshape(B, S, B_WIDTH)

    return jnp.concatenate([oa, ob], axis=-1) @ w_out


def rwkv7_time_mix(h, mu, w_r, w_k, w_v, w_o, w0, w1, w2, a0, a1, a2, g1, g2, k_k, k_a, r_k, ln_w, ln_b):
    B, S, C = h.shape
    H, N = RWKV_HEADS, HEAD_DIM
    xx = jnp.pad(h, ((0, 0), (1, 0), (0, 0)))[:, :-1] - h
    xr, xw, xk, xv, xa, xg = [h + xx * mu[i] for i in range(6)]
    r = xr @ w_r
    k = xk @ w_k
    v = xv @ w_v
    w = -jax.nn.softplus(-(w0 + jnp.tanh(xw @ w1) @ w2)) - 0.5
    decay = jnp.exp(-jnp.exp(w.astype(jnp.float32)))
    a = jax.nn.sigmoid(a0 + (xa @ a1) @ a2)
    g = jax.nn.sigmoid(xg @ g1) @ g2
    kk = (k * k_k).reshape(B, S, H, N).astype(jnp.float32)
    kk = kk / jnp.maximum(jnp.sqrt(jnp.sum(kk * kk, axis=-1, keepdims=True)), 1e-12)
    k = k * (1.0 + (a - 1.0) * k_a)

    def hd(t):
        return t.reshape(B, S, H, N).astype(jnp.float32)

    r_h, k_h, v_h, a_h = hd(r), hd(k), hd(v), hd(a)
    seqs = tuple(jnp.moveaxis(t, 1, 0) for t in (r_h, hd(decay), k_h, v_h, -kk, kk * a_h))

    def step(state, inp):
        r_t, w_t, k_t, v_t, a_t, b_t = inp
        sa = jnp.einsum('bhij,bhj->bhi', state, a_t)
        state = state * w_t[:, :, None, :] + sa[..., None] * b_t[:, :, None, :] + v_t[..., None] * k_t[:, :, None, :]
        return state, jnp.einsum('bhij,bhj->bhi', state, r_t)

    _, y = lax.scan(step, jnp.zeros((B, H, N, N), jnp.float32), seqs)
    y = jnp.moveaxis(y, 0, 1)
    mean = jnp.mean(y, axis=-1, keepdims=True)
    var = jnp.mean(jnp.square(y - mean), axis=-1, keepdims=True)
    yn = ((y - mean) * lax.rsqrt(var + GN_EPS)).reshape(B, S, C) * ln_w + ln_b
    bonus = jnp.sum(r_h * k_h * r_k.astype(jnp.float32), axis=-1, keepdims=True) * v_h
    out = (yn + bonus.reshape(B, S, C)).astype(h.dtype)
    return (out * g) @ w_o


def squared_relu_mlp(h, w1, w2):
    return jnp.square(jax.nn.relu(h @ w1)) @ w2


def setup_inputs(seed: int = 0) -> dict:
    key = jax.random.key(seed)
    ks = iter(jax.random.split(key, 40))
    C = D_MODEL

    def nrm(shape, scale):
        return jax.random.normal(next(ks), shape, jnp.float32) * scale

    def gain(shape, base=1.0):
        return base + nrm(shape, 0.02)

    return {
        "x": nrm((BATCH, SEQ, C), 1.0),
        "norm_mix": gain((DEPTH, C)),
        "norm_mlp": gain((DEPTH, C)),
        "norm_final": gain((C,)),
        "attn_w_in": nrm((N_EVEN, C, QKV_COLS), C ** -0.5),
        "attn_w_out": nrm((N_EVEN, MIX_WIDTH, C), MIX_WIDTH ** -0.5),
        "diff_lambda": nrm((N_EVEN, 4, HEAD_DIM), 0.1),
        "diff_subln": gain((N_EVEN, 2 * HEAD_DIM)),
        "rwkv_mu": jax.random.uniform(next(ks), (N_ODD, 6, C), jnp.float32),
        "rwkv_w_r": nrm((N_ODD, C, C), C ** -0.5),
        "rwkv_w_k": nrm((N_ODD, C, C), C ** -0.5),
        "rwkv_w_v": nrm((N_ODD, C, C), C ** -0.5),
        "rwkv_w_o": nrm((N_ODD, C, C), C ** -0.5),
        "rwkv_w0": jax.random.uniform(next(ks), (N_ODD, C), jnp.float32, -6.0, -1.0),
        "rwkv_w1": nrm((N_ODD, C, DECAY_LORA), C ** -0.5),
        "rwkv_w2": nrm((N_ODD, DECAY_LORA, C), 0.1 * DECAY_LORA ** -0.5),
        "rwkv_a0": nrm((N_ODD, C), 0.1),
        "rwkv_a1": nrm((N_ODD, C, ICLR_LORA), C ** -0.5),
        "rwkv_a2": nrm((N_ODD, ICLR_LORA, C), 0.1 * ICLR_LORA ** -0.5),
        "rwkv_g1": nrm((N_ODD, C, GATE_LORA), C ** -0.5),
        "rwkv_g2": nrm((N_ODD, GATE_LORA, C), GATE_LORA ** -0.5),
        "rwkv_k_k": gain((N_ODD, C), 0.85),
        "rwkv_k_a": gain((N_ODD, C)),
        "rwkv_r_k": nrm((N_ODD, RWKV_HEADS, HEAD_DIM), 0.1),
        "rwkv_ln_w": gain((N_ODD, C)),
        "rwkv_ln_b": nrm((N_ODD, C), 0.02),
        "mlp_w1": nrm((DEPTH, C, D_FF), C ** -0.5),
        "mlp_w2": nrm((DEPTH, D_FF, C), D_FF ** -0.5),
    }


def reference(x, norm_mix, norm_mlp, norm_final, attn_w_in, attn_w_out, diff_lambda, diff_subln,
              rwkv_mu, rwkv_w_r, rwkv_w_k, rwkv_w_v, rwkv_w_o, rwkv_w0, rwkv_w1, rwkv_w2,
              rwkv_a0, rwkv_a1, rwkv_a2, rwkv_g1, rwkv_g2, rwkv_k_k, rwkv_k_a, rwkv_r_k,
              rwkv_ln_w, rwkv_ln_b, mlp_w1, mlp_w2):
    h = x
    for layer in range(DEPTH):
        j = layer // 2
        hn = rmsnorm(h, norm_mix[layer])
        if layer % 2 == 0:
            lam_init = 0.8 - 0.6 * math.exp(-0.3 * layer)
            mix = hybrid_attention(hn, attn_w_in[j], attn_w_out[j], diff_lambda[j], diff_subln[j], lam_init)
        else:
            mix = rwkv7_time_mix(hn, rwkv_mu[j], rwkv_w_r[j], rwkv_w_k[j], rwkv_w_v[j], rwkv_w_o[j],
                                 rwkv_w0[j], rwkv_w1[j], rwkv_w2[j], rwkv_a0[j], rwkv_a1[j], rwkv_a2[j],
                                 rwkv_g1[j], rwkv_g2[j], rwkv_k_k[j], rwkv_k_a[j], rwkv_r_k[j],
                                 rwkv_ln_w[j], rwkv_ln_b[j])
        h = h + mix
        h = h + squared_relu_mlp(rmsnorm(h, norm_mlp[layer]), mlp_w1[layer], mlp_w2[layer])
    return rmsnorm(h, norm_final)
```

```python
import functools
import math

import jax
import jax.numpy as jnp
from jax import lax
from jax.experimental import pallas as pl
from jax.experimental.pallas import tpu as pltpu

F32 = jnp.float32
BF16 = jnp.bfloat16

D_MODEL = 1024
HEAD_DIM = 64
ROT_DIM = HEAD_DIM // 4
ROPE_THETA = 500000.0
BLK = 128
A_PATTERNS = ((128, 1), (512, 4), (2048, 16))
A_WIDTH = 512
B_HEADS = 4
B_WIDTH = 512
QKV_COLS = 3072
GN_EPS = 64e-5
D_FF = 4 * D_MODEL
EPS = 1e-5
GATE_LORA_PAD = 256

LANES = 128
N_LANE_BLOCKS = D_MODEL // LANES
NEG = -1e30
VMEM_LIMIT = 56 * 1024 * 1024

ROW_TILE = 512
RWKV_ROW_TILE = 256
RWKV_CHUNK = 64
RWKV_CHUNKS_PER_STEP = 4
DIFF_TQ = 256
DIFF_TK = 256


def _cparams(*semantics):
    return pltpu.CompilerParams(dimension_semantics=semantics, vmem_limit_bytes=VMEM_LIMIT)


def _dot(a, b):
    return jnp.dot(a, b, preferred_element_type=F32)


def _dot_nt(a, b):
    return lax.dot_general(a, b, (((1,), (1,)), ((), ())), preferred_element_type=F32)


def _split2(x):
    hi = x.astype(BF16)
    lo = (x - hi.astype(F32)).astype(BF16)
    return hi, lo


def _dot3(a, b):
    ah, al = _split2(a)
    bh, bl = _split2(b)
    return _dot(ah, bh) + (_dot(ah, bl) + _dot(al, bh))


def _rms(x, g):
    return x * lax.rsqrt(jnp.mean(x * x, axis=-1, keepdims=True) + EPS) * g


def _head_mask(shape):
    return lax.broadcasted_iota(jnp.int32, shape, len(shape) - 1) < HEAD_DIM


def _full_spec(shape):
    nd = len(shape)
    return pl.BlockSpec(shape, lambda *_: (0,) * nd)


def _rope_tables(S):
    half = ROT_DIM // 2
    inv_freq = ROPE_THETA ** (-jnp.arange(half, dtype=F32) / half)
    ang = jnp.arange(S, dtype=F32)[:, None] * inv_freq[None, :]
    d = jnp.arange(LANES) % HEAD_DIM
    ang_l = ang[:, d % half]
    lo = (d < half)[None, :]
    hi = ((d >= half) & (d < ROT_DIM))[None, :]
    cos = jnp.where(lo | hi, jnp.cos(ang_l), 1.0)
    sin_up = jnp.where(hi, jnp.sin(ang_l), 0.0)
    sin_dn = jnp.where(lo, -jnp.sin(ang_l), 0.0)
    return cos, sin_up, sin_dn


_QKV_GROUPS = ((True, HEAD_DIM ** -0.5), (True, 1.0), (False, 1.0),
               (True, HEAD_DIM ** -0.5), (True, 1.0), (False, 1.0))


def _qkv_kernel(x_ref, g_ref, w_ref, cos_ref, su_ref, sd_ref, o_ref):
    hn = _rms(x_ref[0], g_ref[...]).astype(BF16)
    cos, su, sd = cos_ref[...], su_ref[...], sd_ref[...]
    half = ROT_DIM // 2
    for j, (rot, scale) in enumerate(_QKV_GROUPS):
        y = _dot(hn, w_ref[:, j * 512:(j + 1) * 512])
        for c in range(4):
            t = y[:, c * LANES:(c + 1) * LANES]
            if rot:
                t = t * cos + pltpu.roll(t, half, 1) * su + pltpu.roll(t, LANES - half, 1) * sd
            if scale != 1.0:
                t = t * scale
            o_ref[0, 4 * j + c] = t.astype(BF16)


def _qkv_proj(x, g, w_in):
    B, S, C = x.shape
    tm = ROW_TILE
    cos, su, sd = _rope_tables(S)
    nblk = QKV_COLS // LANES
    tab = pl.BlockSpec((tm, LANES), lambda b, i: (i, 0))
    return pl.pallas_call(
        _qkv_kernel,
        grid=(B, S // tm),
        in_specs=[pl.BlockSpec((1, tm, C), lambda b, i: (b, i, 0)),
                  _full_spec((1, C)), _full_spec((C, QKV_COLS)), tab, tab, tab],
        out_specs=pl.BlockSpec((1, nblk, tm, LANES), lambda b, i: (b, 0, i, 0)),
        out_shape=jax.ShapeDtypeStruct((B, nblk, S, LANES), BF16),
        compiler_params=_cparams("parallel", "parallel"),
    )(x, g.reshape(1, C), w_in.astype(BF16), cos, su, sd)


def _attn_a_kernel(q_ref, k_ref, v_ref, o_ref, qf, kf, vf, acc_s, m_s, l_s, *, S, pad):
    CH = 512

    def load(i, _):
        sl = pl.ds(pl.multiple_of(i * CH, CH), CH)
        dst = pl.ds(pl.multiple_of(pad + i * CH, CH), CH)
        qf[sl, :] = q_ref[0, 0, sl, :].astype(F32)
        kf[dst, :] = k_ref[0, 0, sl, :].astype(F32)
        vf[dst, :] = v_ref[0, 0, sl, :].astype(F32)
        return 0

    lax.fori_loop(0, S // CH, load, 0)
    kf[pl.ds(0, pad), :] = jnp.zeros((pad, LANES), F32)
    vf[pl.ds(0, pad), :] = jnp.zeros((pad, LANES), F32)

    h0 = _head_mask((BLK, LANES))
    qi = lax.broadcasted_iota(jnp.int32, (BLK, 2 * BLK), 0)
    kc = lax.broadcasted_iota(jnp.int32, (BLK, 2 * BLK), 1)
    dist = BLK + qi - kc
    band = (dist >= 0) & (dist <= BLK)
    bias = jnp.where(band, 0.0, NEG)
    bias_first = jnp.where(band & (kc >= BLK), 0.0, NEG)

    for p, (window, d) in enumerate(A_PATTERNS):
        assert window // d == BLK
        nb = S // (d * BLK)

        def unit(u, _, p=p, d=d):
            r = u % d
            n = u // d
            rows_q = pl.ds(r + n * (BLK * d), BLK, stride=d) if d > 1 else pl.ds(pl.multiple_of(n * BLK, BLK), BLK)
            k0 = pad + r + (n - 1) * (BLK * d)
            rows_k = pl.ds(k0, 2 * BLK, stride=d) if d > 1 else pl.ds(pl.multiple_of(k0, BLK), 2 * BLK)
            q = qf[rows_q, :]
            kb = kf[rows_k, :].astype(BF16)
            vb = vf[rows_k, :].astype(BF16)
            bias_n = jnp.where(n == 0, bias_first, bias)
            parts = []
            for first in (True, False):
                qh = jnp.where(h0 if first else jnp.logical_not(h0), q, 0.0).astype(BF16)
                s = _dot_nt(qh, kb) + bias_n
                mh = jnp.max(s, axis=-1, keepdims=True)
                ph = jnp.exp(s - mh)
                lh = jnp.sum(ph, axis=-1, keepdims=True)
                parts.append((mh, lh, _dot(ph.astype(BF16), vb)))
            m_new = jnp.where(h0, parts[0][0], parts[1][0])
            l_new = jnp.where(h0, parts[0][1], parts[1][1])
            a_new = jnp.where(h0, parts[0][2], parts[1][2])
            if p == 0:
                m_s[rows_q, :] = m_new
                l_s[rows_q, :] = l_new
                acc_s[rows_q, :] = a_new
            else:
                m_old = m_s[rows_q, :]
                m_tot = jnp.maximum(m_old, m_new)
                w_old = jnp.exp(m_old - m_tot)
                w_new = jnp.exp(m_new - m_tot)
                m_s[rows_q, :] = m_tot
                l_s[rows_q, :] = l_s[rows_q, :] * w_old + l_new * w_new
                acc_s[rows_q, :] = acc_s[rows_q, :] * w_old + a_new * w_new
            return 0

        lax.fori_loop(0, d * nb, unit, 0)

    def store(i, _):
        sl = pl.ds(pl.multiple_of(i * CH, CH), CH)
        o_ref[0, 0, sl, :] = (acc_s[sl, :] / l_s[sl, :]).astype(BF16)
        return 0

    lax.fori_loop(0, S // CH, store, 0)


def _attn_a(qkv):
    B, _, S, _ = qkv.shape
    npair = A_WIDTH // LANES
    pad = BLK * A_PATTERNS[-1][1]
    assert S % pad == 0 and S // pad >= 2
    blk = lambda off: pl.BlockSpec((1, 1, S, LANES), lambda b, h: (b, off + h, 0, 0))
    return pl.pallas_call(
        functools.partial(_attn_a_kernel, S=S, pad=pad),
        grid=(B, npair),
        in_specs=[blk(0), blk(npair), blk(2 * npair)],
        out_specs=blk(0),
        out_shape=jax.ShapeDtypeStruct((B, npair, S, LANES), BF16),
        scratch_shapes=[pltpu.VMEM((S, LANES), F32), pltpu.VMEM((S + pad, LANES), F32),
                        pltpu.VMEM((S + pad, LANES), F32), pltpu.VMEM((S, LANES), F32),
                        pltpu.VMEM((S, LANES), F32), pltpu.VMEM((S, LANES), F32)],
        compiler_params=_cparams("parallel", "parallel"),
    )(qkv, qkv, qkv)


def _attn_b_kernel(lam_ref, gain_ref, q_ref, k_ref, v_ref, o_ref, *, S, lam_init):
    TQ, TK = DIFF_TQ, DIFF_TK
    assert TQ == TK
    lp = lam_ref[...]
    lam = (jnp.exp(jnp.sum(lp[0:1] * lp[1:2], axis=-1, keepdims=True))
           - jnp.exp(jnp.sum(lp[2:3] * lp[3:4], axis=-1, keepdims=True)) + lam_init)
    h0 = _head_mask((TQ, LANES))
    row = lax.broadcasted_iota(jnp.int32, (2 * TQ, TK), 0)
    col = lax.broadcasted_iota(jnp.int32, (2 * TQ, TK), 1)
    causal = jnp.where(row >= TQ, row - TQ, row) >= col
    zero = jnp.zeros((), BF16)

    def q_block(qi, _):
        q = q_ref[0, 0, pl.ds(pl.multiple_of(qi * TQ, TQ), TQ), :]
        qs = jnp.concatenate([jnp.where(h0, q, zero), jnp.where(h0, zero, q)], axis=0)

        def kv_step(kj, carry, masked):
            m, l, acc = carry
            rows = pl.ds(pl.multiple_of(kj * TK, TK), TK)
            kb = k_ref[0, 0, rows, :]
            vb = v_ref[0, 0, rows, :]
            s = _dot_nt(qs, kb)
            if masked:
                s = jnp.where(causal, s, NEG)
            m_new = jnp.maximum(m, jnp.max(s, axis=-1, keepdims=True))
            alpha = jnp.exp(m - m_new)
            p = jnp.exp(s - m_new)
            l = alpha * l + jnp.sum(p, axis=-1, keepdims=True)
            acc = alpha * acc + _dot(p.astype(BF16), vb)
            return m_new, l, acc

        init = (jnp.full((2 * TQ, 1), NEG, F32), jnp.zeros((2 * TQ, 1), F32),
                jnp.zeros((2 * TQ, LANES), F32))
        carry = lax.fori_loop(0, qi, functools.partial(kv_step, masked=False), init)
        _, l, acc = kv_step(qi, carry, True)
        o = acc / l
        o = o[:TQ] - lam * o[TQ:]
        o = _rms(o, gain_ref[...]) * (1.0 - lam_init)
        o_ref[0, 0, pl.ds(pl.multiple_of(qi * TQ, TQ), TQ), :] = o.astype(BF16)
        return 0

    lax.fori_loop(0, S // TQ, q_block, 0)


def _attn_b(qkv, diff_lambda, subln, lam_init):
    B, _, S, _ = qkv.shape
    base = 3 * A_WIDTH // LANES
    blk = lambda off: pl.BlockSpec((1, 1, S, LANES), lambda b, h: (b, off + h, 0, 0))
    return pl.pallas_call(
        functools.partial(_attn_b_kernel, S=S, lam_init=lam_init),
        grid=(B, B_HEADS),
        in_specs=[_full_spec((4, HEAD_DIM)), _full_spec((1, 2 * HEAD_DIM)),
                  blk(base), blk(base + B_HEADS), blk(base + 2 * B_HEADS)],
        out_specs=blk(0),
        out_shape=jax.ShapeDtypeStruct((B, B_HEADS, S, LANES), BF16),
        compiler_params=_cparams("parallel", "parallel"),
    )(diff_lambda, subln.reshape(1, 2 * HEAD_DIM), qkv, qkv, qkv)


def _mix_mlp_kernel(*refs, n_act, final_norm):
    res_ref = refs[0]
    act_refs = refs[1:1 + n_act]
    wo_ref, g_ref, w1_ref, w2_ref = refs[1 + n_act:5 + n_act]
    gf_ref = refs[5 + n_act] if final_norm else None
    o_ref = refs[-1]
    act = jnp.concatenate([a[0, c] for a in act_refs for c in range(a.shape[1])], axis=-1)
    h = res_ref[0] + _dot(act, wo_ref[...])
    hn = _rms(h, g_ref[...]).astype(BF16)
    fc = D_MODEL
    acc = jnp.zeros_like(h)
    for f in range(D_FF // fc):
        a = _dot(hn, w1_ref[:, f * fc:(f + 1) * fc])
        a = jnp.square(jnp.maximum(a, 0.0)).astype(BF16)
        acc = acc + _dot(a, w2_ref[f * fc:(f + 1) * fc, :])
    h = h + acc
    if final_norm:
        h = _rms(h, gf_ref[...])
    o_ref[0] = h


def _mix_mlp(res, acts, w_o, g, w1, w2, g_final=None):
    B, S, C = res.shape
    tm = ROW_TILE
    final_norm = g_final is not None
    row = pl.BlockSpec((1, tm, C), lambda b, i: (b, i, 0))
    in_specs = [row]
    for a in acts:
        in_specs.append(pl.BlockSpec((1, a.shape[1], tm, LANES), lambda b, i: (b, 0, i, 0)))
    in_specs += [_full_spec(w_o.shape), _full_spec((1, C)), _full_spec(w1.shape), _full_spec(w2.shape)]
    args = [res, *acts, w_o.astype(BF16), g.reshape(1, C), w1.astype(BF16), w2.astype(BF16)]
    if final_norm:
        in_specs.append(_full_spec((1, C)))
        args.append(g_final.reshape(1, C))
    return pl.pallas_call(
        functools.partial(_mix_mlp_kernel, n_act=len(acts), final_norm=final_norm),
        grid=(B, S // tm),
        in_specs=in_specs,
        out_specs=row,
        out_shape=jax.ShapeDtypeStruct((B, S, C), F32),
        compiler_params=_cparams("parallel", "parallel"),
    )(*args)


def _rwkv_proj_kernel(x_ref, xp_ref, g_ref, mu_ref, wr_ref, wk_ref, wv_ref, w1_ref, w2_ref,
                      a1_ref, a2_ref, g1_ref, g2_ref, w0_ref, a0_ref, kk_ref, ka_ref,
                      r_out, lw_out, k_out, v_out, kk_out, a_out, g_out):
    i = pl.program_id(1)
    g = g_ref[...]
    hn = _rms(x_ref[0], g)
    tm = hn.shape[0]
    last = _rms(xp_ref[0], g)[7:8, :]
    last = jnp.where(i == 0, 0.0, last)
    row = lax.broadcasted_iota(jnp.int32, hn.shape, 0)
    prev = jnp.where(row == 0, last, pltpu.roll(hn, 1, 0))
    xx = prev - hn

    def mixed(j):
        return (hn + xx * mu_ref[j:j + 1, :]).astype(BF16)

    r = _dot(mixed(0), wr_ref[...])
    wl = jnp.tanh(_dot(mixed(1), w1_ref[...])).astype(BF16)
    u = w0_ref[...] + _dot(wl, w2_ref[...])
    w = -(jnp.maximum(-u, 0.0) + jnp.log(1.0 + jnp.exp(-jnp.abs(u)))) - 0.5
    lw = -jnp.exp(w)
    k = _dot(mixed(2), wk_ref[...])
    v = _dot(mixed(3), wv_ref[...])
    al = _dot(mixed(4), a1_ref[...]).astype(BF16)
    a = jax.nn.sigmoid(a0_ref[...] + _dot(al, a2_ref[...]))
    gl = jax.nn.sigmoid(_dot(mixed(5), g1_ref[...])).astype(BF16)
    gate = _dot(gl, g2_ref[...])
    kk = k * kk_ref[...]
    k = k * (1.0 + (a - 1.0) * ka_ref[...])

    ri = lax.broadcasted_iota(jnp.int32, (LANES, LANES), 0) // HEAD_DIM
    ci = lax.broadcasted_iota(jnp.int32, (LANES, LANES), 1) // HEAD_DIM
    seg = jnp.where(ri == ci, 1.0, 0.0).astype(BF16)
    for c in range(N_LANE_BLOCKS):
        sl = slice(c * LANES, (c + 1) * LANES)
        kc = kk[:, sl]
        hi, lo = _split2(kc * kc)
        ss = _dot(hi, seg) + _dot(lo, seg)
        kk_out[0, c] = (kc / jnp.maximum(jnp.sqrt(ss), 1e-12)).astype(BF16)
        r_out[0, c] = r[:, sl].astype(BF16)
        lw_out[0, c] = lw[:, sl]
        k_out[0, c] = k[:, sl].astype(BF16)
        v_out[0, c] = v[:, sl].astype(BF16)
        a_out[0, c] = a[:, sl].astype(BF16)
        g_out[0, c] = gate[:, sl].astype(BF16)


def _rwkv_proj(h, g, mu, w_r, w_k, w_v, w1, w2, a1, a2, g1, g2, w0, a0, k_k, k_a):
    B, S, C = h.shape
    tm = RWKV_ROW_TILE
    glr = g1.shape[1]
    g1p = jnp.pad(g1, ((0, 0), (0, GATE_LORA_PAD - glr))).astype(BF16)
    g2p = jnp.pad(g2, ((0, GATE_LORA_PAD - glr), (0, 0))).astype(BF16)
    vec = lambda t: t.reshape(1, C)
    args = [h, h, vec(g), mu, w_r.astype(BF16), w_k.astype(BF16), w_v.astype(BF16),
            w1.astype(BF16), w2.astype(BF16), a1.astype(BF16), a2.astype(BF16), g1p, g2p,
            vec(w0), vec(a0), vec(k_k), vec(k_a)]
    in_specs = [pl.BlockSpec((1, tm, C), lambda b, i: (b, i, 0)),
                pl.BlockSpec((1, 8, C), lambda b, i: (b, jnp.maximum(i * (tm // 8) - 1, 0), 0))]
    in_specs += [_full_spec(a.shape) for a in args[2:]]
    out_spec = pl.BlockSpec((1, N_LANE_BLOCKS, tm, LANES), lambda b, i: (b, 0, i, 0))
    shp = lambda dt: jax.ShapeDtypeStruct((B, N_LANE_BLOCKS, S, LANES), dt)
    return pl.pallas_call(
        _rwkv_proj_kernel,
        grid=(B, S // tm),
        in_specs=in_specs,
        out_specs=[out_spec] * 7,
        out_shape=[shp(BF16), shp(F32), shp(BF16), shp(BF16), shp(BF16), shp(BF16), shp(BF16)],
        compiler_params=_cparams("parallel", "parallel"),
    )(*args)


def _tri_inverse(n_mat):
    L = n_mat.shape[0]
    eye = jnp.where(lax.broadcasted_iota(jnp.int32, (L, L), 0) == lax.broadcasted_iota(jnp.int32, (L, L), 1),
                    1.0, 0.0)
    t = eye + n_mat
    pw = n_mat
    for _ in range(int(math.log2(L)) - 1):
        pw = _dot3(pw, pw)
        t = t + _dot3(pw, t)
    return t


def _rwkv_kernel(r_ref, lw_ref, k_ref, v_ref, kk_ref, a_ref, g_ref, rk_ref, lnw_ref, lnb_ref,
                 o_ref, z_s, *, L, CB):
    @pl.when(pl.program_id(2) == 0)
    def _():
        z_s[...] = jnp.zeros_like(z_s)

    r_all = r_ref[0, 0].astype(F32)
    lw_all = lw_ref[0, 0]
    k_all = k_ref[0, 0].astype(F32)
    v_all = v_ref[0, 0].astype(F32)
    kk_all = kk_ref[0, 0].astype(F32)
    a_all = a_ref[0, 0].astype(F32)

    h0 = _head_mask((L, LANES))
    h0_2 = _head_mask((2 * L, LANES))
    ti = lax.broadcasted_iota(jnp.int32, (L, L), 0)
    si = lax.broadcasted_iota(jnp.int32, (L, L), 1)
    strict = ti > si
    incl = ti >= si
    tri = jnp.where(incl, 1.0, 0.0).astype(BF16)
    ri = lax.broadcasted_iota(jnp.int32, (LANES, LANES), 0)
    ci = lax.broadcasted_iota(jnp.int32, (LANES, LANES), 1)
    same_head = (ri // HEAD_DIM) == (ci // HEAD_DIM)
    diag = ri == ci
    seg = jnp.where(same_head, 1.0, 0.0).astype(BF16)

    def sel(x0, x1):
        return jnp.where(_head_mask(x0.shape), x0, x1)

    z = z_s[...]
    ys = []
    for c in range(CB):
        sl = slice(c * L, (c + 1) * L)
        r, lw, k, v, kk, a = (t[sl] for t in (r_all, lw_all, k_all, v_all, kk_all, a_all))
        hi = lw.astype(BF16)
        r1 = lw - hi.astype(F32)
        mid = r1.astype(BF16)
        lo = (r1 - mid.astype(F32)).astype(BF16)
        cs = _dot(tri, hi) + (_dot(tri, mid) + _dot(tri, lo))
        tot = cs[L - 1:L, :]
        p_in = jnp.exp(cs)
        p_prev = jnp.exp(cs - lw)
        p_inv = jnp.exp(-cs)
        p_rest = jnp.exp(tot - cs)
        p_tot = jnp.exp(tot)
        kka = kk * a
        a_t = -(kk * p_prev)
        b_t = (kka * p_inv).astype(BF16)
        k_t = (k * p_inv).astype(BF16)
        r_t = r * p_in
        b_h = kka * p_rest
        k_h = k * p_rest
        vb = v.astype(BF16)

        ar = jnp.concatenate([a_t, r_t], axis=0)
        heads = []
        for first in (True, False):
            arh = jnp.where(h0_2 if first else jnp.logical_not(h0_2), ar, 0.0).astype(BF16)
            m_b = _dot_nt(arh, b_t)
            m_k = _dot_nt(arh, k_t)
            a_ab = jnp.where(strict, m_b[:L], 0.0)
            a_rb = jnp.where(incl, m_b[L:], 0.0).astype(BF16)
            a_ak = jnp.where(strict, m_k[:L], 0.0).astype(BF16)
            a_rk = jnp.where(incl, m_k[L:], 0.0).astype(BF16)
            t_inv = _tri_inverse(a_ab)
            heads.append((t_inv, a_rb, a_rk, _dot(a_ak, vb)))
        akv = sel(heads[0][3], heads[1][3])
        x = jnp.concatenate([a_t, akv], axis=1)
        tx = [_dot3(hd[0], x) for hd in heads]
        w = sel(tx[0][:, :LANES], tx[1][:, :LANES])
        uv = sel(tx[0][:, LANES:], tx[1][:, LANES:])
        wu = jnp.concatenate([w, uv], axis=1).astype(BF16)
        rbw = [_dot(hd[1], wu) for hd in heads]
        rkv = [_dot(hd[2], vb) for hd in heads]
        rw = r_t + sel(rbw[0][:, :LANES], rbw[1][:, :LANES])
        yv = sel(rbw[0][:, LANES:] + rkv[0], rbw[1][:, LANES:] + rkv[1])
        bht = b_h.T.astype(BF16)
        kht = k_h.T.astype(BF16)
        gw = _dot(bht, wu)
        g_mat = jnp.where(same_head, gw[:, :LANES], 0.0) + jnp.where(diag, p_tot, 0.0)
        h_mat = jnp.where(same_head, gw[:, LANES:] + _dot(kht, vb), 0.0)

        ys.append(_dot3(rw, z) + yv)
        z = _dot3(g_mat, z) + h_mat
    z_s[...] = z

    y = jnp.concatenate(ys, axis=0)

    def head_sum(t):
        hi, lo = _split2(t)
        return _dot(hi, seg) + _dot(lo, seg)

    inv_n = 1.0 / HEAD_DIM
    yc = y - head_sum(y) * inv_n
    var = head_sum(yc * yc) * inv_n
    yn = yc * lax.rsqrt(var + GN_EPS) * lnw_ref[0] + lnb_ref[0]
    bonus = head_sum(r_all * k_all * rk_ref[0]) * v_all
    o_ref[0, 0] = ((yn + bonus) * g_ref[0, 0].astype(F32)).astype(BF16)


def _rwkv_mix(r, lw, k, v, kk, a, gate, r_k, ln_w, ln_b):
    B, nb, S, _ = r.shape
    L, CB = RWKV_CHUNK, RWKV_CHUNKS_PER_STEP
    rows = L * CB
    blk = pl.BlockSpec((1, 1, rows, LANES), lambda b, h, c: (b, h, c, 0))
    par = pl.BlockSpec((1, 1, LANES), lambda b, h, c: (h, 0, 0))
    pv = lambda t: t.reshape(nb, 1, LANES)
    return pl.pallas_call(
        functools.partial(_rwkv_kernel, L=L, CB=CB),
        grid=(B, nb, S // rows),
        in_specs=[blk] * 7 + [par] * 3,
        out_specs=blk,
        out_shape=jax.ShapeDtypeStruct((B, nb, S, LANES), BF16),
        scratch_shapes=[pltpu.VMEM((LANES, LANES), F32)],
        compiler_params=_cparams("parallel", "parallel", "arbitrary"),
    )(r, lw, k, v, kk, a, gate, pv(r_k), pv(ln_w), pv(ln_b))


def kernel(x, norm_mix, norm_mlp, norm_final, attn_w_in, attn_w_out, diff_lambda, diff_subln, rwkv_mu, rwkv_w_r, rwkv_w_k, rwkv_w_v, rwkv_w_o, rwkv_w0, rwkv_w1, rwkv_w2, rwkv_a0, rwkv_a1, rwkv_a2, rwkv_g1, rwkv_g2, rwkv_k_k, rwkv_k_a, rwkv_r_k, rwkv_ln_w, rwkv_ln_b, mlp_w1, mlp_w2):
    lam_init = 0.8 - 0.6 * math.exp(-0.3 * 0)
    qkv = _qkv_proj(x, norm_mix[0], attn_w_in[0])
    oa = _attn_a(qkv)
    ob = _attn_b(qkv, diff_lambda[0], diff_subln[0], lam_init)
    h = _mix_mlp(x, [oa, ob], attn_w_out[0], norm_mlp[0], mlp_w1[0], mlp_w2[0])
    r, lw, k, v, kk, a, gate = _rwkv_proj(
        h, norm_mix[1], rwkv_mu[0], rwkv_w_r[0], rwkv_w_k[0], rwkv_w_v[0], rwkv_w1[0], rwkv_w2[0],
        rwkv_a1[0], rwkv_a2[0], rwkv_g1[0], rwkv_g2[0], rwkv_w0[0], rwkv_a0[0], rwkv_k_k[0], rwkv_k_a[0])
    mix = _rwkv_mix(r, lw, k, v, kk, a, gate, rwkv_r_k[0], rwkv_ln_w[0], rwkv_ln_b[0])
    return _mix_mlp(h, [mix], rwkv_w_o[0], norm_mlp[1], mlp_w1[1], mlp_w2[1], g_final=norm_final)
```

```python
import functools
import math

import jax
import jax.numpy as jnp
from jax import lax
from jax.experimental import pallas as pl
from jax.experimental.pallas import tpu as pltpu

F32 = jnp.float32
BF16 = jnp.bfloat16

D_MODEL = 1024
HEAD_DIM = 64
ROT_DIM = HEAD_DIM // 4
ROPE_THETA = 500000.0
BLK = 128
A_PATTERNS = ((128, 1), (512, 4), (2048, 16))
A_WIDTH = 512
B_HEADS = 4
B_WIDTH = 512
QKV_COLS = 3072
GN_EPS = 64e-5
D_FF = 4 * D_MODEL
EPS = 1e-5
GATE_LORA_PAD = 256

LANES = 128
N_LANE_BLOCKS = D_MODEL // LANES
NEG = -1e30
VMEM_LIMIT = 56 * 1024 * 1024

ROW_TILE = 512
RWKV_ROW_TILE = 256
RWKV_CHUNK = 64
RWKV_CHUNKS_PER_STEP = 2
RWKV_TRI_PASSES = 1
RWKV_STATE_PASSES = 3
DIFF_TQ = 256
DIFF_TK = 256


def _cparams(*semantics):
    return pltpu.CompilerParams(dimension_semantics=semantics, vmem_limit_bytes=VMEM_LIMIT)


def _dot(a, b):
    return jnp.dot(a, b, preferred_element_type=F32)


def _dot_nt(a, b):
    return lax.dot_general(a, b, (((1,), (1,)), ((), ())), preferred_element_type=F32)


def _split2(x):
    hi = x.astype(BF16)
    lo = (x - hi.astype(F32)).astype(BF16)
    return hi, lo


def _dot3(a, b):
    ah, al = _split2(a)
    bh, bl = _split2(b)
    return _dot(ah, bh) + (_dot(ah, bl) + _dot(al, bh))


def _rms(x, g):
    return x * lax.rsqrt(jnp.mean(x * x, axis=-1, keepdims=True) + EPS) * g


def _head_mask(shape):
    return lax.broadcasted_iota(jnp.int32, shape, len(shape) - 1) < HEAD_DIM


def _full_spec(shape):
    nd = len(shape)
    return pl.BlockSpec(shape, lambda *_: (0,) * nd)


def _rope_tables(S):
    half = ROT_DIM // 2
    inv_freq = ROPE_THETA ** (-jnp.arange(half, dtype=F32) / half)
    ang = jnp.arange(S, dtype=F32)[:, None] * inv_freq[None, :]
    d = jnp.arange(LANES) % HEAD_DIM
    ang_l = ang[:, d % half]
    lo = (d < half)[None, :]
    hi = ((d >= half) & (d < ROT_DIM))[None, :]
    cos = jnp.where(lo | hi, jnp.cos(ang_l), 1.0)
    sin_up = jnp.where(hi, jnp.sin(ang_l), 0.0)
    sin_dn = jnp.where(lo, -jnp.sin(ang_l), 0.0)
    return cos, sin_up, sin_dn


_QKV_GROUPS = ((True, HEAD_DIM ** -0.5), (True, 1.0), (False, 1.0),
               (True, HEAD_DIM ** -0.5), (True, 1.0), (False, 1.0))


def _qkv_kernel(x_ref, g_ref, w_ref, cos_ref, su_ref, sd_ref, o_ref):
    hn = _rms(x_ref[0], g_ref[...]).astype(BF16)
    cos, su, sd = cos_ref[...], su_ref[...], sd_ref[...]
    half = ROT_DIM // 2
    for j, (rot, scale) in enumerate(_QKV_GROUPS):
        y = _dot(hn, w_ref[:, j * 512:(j + 1) * 512])
        for c in range(4):
            t = y[:, c * LANES:(c + 1) * LANES]
            if rot:
                t = t * cos + pltpu.roll(t, half, 1) * su + pltpu.roll(t, LANES - half, 1) * sd
            if scale != 1.0:
                t = t * scale
            o_ref[0, 4 * j + c] = t.astype(BF16)


def _qkv_proj(x, g, w_in):
    B, S, C = x.shape
    tm = ROW_TILE
    cos, su, sd = _rope_tables(S)
    nblk = QKV_COLS // LANES
    tab = pl.BlockSpec((tm, LANES), lambda b, i: (i, 0))
    return pl.pallas_call(
        _qkv_kernel,
        grid=(B, S // tm),
        in_specs=[pl.BlockSpec((1, tm, C), lambda b, i: (b, i, 0)),
                  _full_spec((1, C)), _full_spec((C, QKV_COLS)), tab, tab, tab],
        out_specs=pl.BlockSpec((1, nblk, tm, LANES), lambda b, i: (b, 0, i, 0)),
        out_shape=jax.ShapeDtypeStruct((B, nblk, S, LANES), BF16),
        compiler_params=_cparams("parallel", "parallel"),
    )(x, g.reshape(1, C), w_in.astype(BF16), cos, su, sd)


def _attn_a_kernel(q_ref, k_ref, v_ref, o_ref, qf, kf, vf, acc_s, m_s, l_s, *, S, pad):
    CH = 512

    def load(i, _):
        sl = pl.ds(pl.multiple_of(i * CH, CH), CH)
        dst = pl.ds(pl.multiple_of(pad + i * CH, CH), CH)
        qf[sl, :] = q_ref[0, 0, sl, :].astype(F32)
        kf[dst, :] = k_ref[0, 0, sl, :].astype(F32)
        vf[dst, :] = v_ref[0, 0, sl, :].astype(F32)
        return 0

    lax.fori_loop(0, S // CH, load, 0)
    kf[pl.ds(0, pad), :] = jnp.zeros((pad, LANES), F32)
    vf[pl.ds(0, pad), :] = jnp.zeros((pad, LANES), F32)

    h0 = _head_mask((BLK, LANES))
    qi = lax.broadcasted_iota(jnp.int32, (BLK, 2 * BLK), 0)
    kc = lax.broadcasted_iota(jnp.int32, (BLK, 2 * BLK), 1)
    dist = BLK + qi - kc
    band = (dist >= 0) & (dist <= BLK)
    bias = jnp.where(band, 0.0, NEG)
    bias_first = jnp.where(band & (kc >= BLK), 0.0, NEG)

    for p, (window, d) in enumerate(A_PATTERNS):
        assert window // d == BLK
        nb = S // (d * BLK)

        def unit(u, _, p=p, d=d):
            r = u % d
            n = u // d
            rows_q = pl.ds(r + n * (BLK * d), BLK, stride=d) if d > 1 else pl.ds(pl.multiple_of(n * BLK, BLK), BLK)
            k0 = pad + r + (n - 1) * (BLK * d)
            rows_k = pl.ds(k0, 2 * BLK, stride=d) if d > 1 else pl.ds(pl.multiple_of(k0, BLK), 2 * BLK)
            q = qf[rows_q, :]
            kb = kf[rows_k, :].astype(BF16)
            vb = vf[rows_k, :].astype(BF16)
            bias_n = jnp.where(n == 0, bias_first, bias)
            parts = []
            for first in (True, False):
                qh = jnp.where(h0 if first else jnp.logical_not(h0), q, 0.0).astype(BF16)
                s = _dot_nt(qh, kb) + bias_n
                mh = jnp.max(s, axis=-1, keepdims=True)
                ph = jnp.exp(s - mh)
                lh = jnp.sum(ph, axis=-1, keepdims=True)
                parts.append((mh, lh, _dot(ph.astype(BF16), vb)))
            m_new = jnp.where(h0, parts[0][0], parts[1][0])
            l_new = jnp.where(h0, parts[0][1], parts[1][1])
            a_new = jnp.where(h0, parts[0][2], parts[1][2])
            if p == 0:
                m_s[rows_q, :] = m_new
                l_s[rows_q, :] = l_new
                acc_s[rows_q, :] = a_new
            else:
                m_old = m_s[rows_q, :]
                m_tot = jnp.maximum(m_old, m_new)
                w_old = jnp.exp(m_old - m_tot)
                w_new = jnp.exp(m_new - m_tot)
                m_s[rows_q, :] = m_tot
                l_s[rows_q, :] = l_s[rows_q, :] * w_old + l_new * w_new
                acc_s[rows_q, :] = acc_s[rows_q, :] * w_old + a_new * w_new
            return 0

        lax.fori_loop(0, d * nb, unit, 0)

    def store(i, _):
        sl = pl.ds(pl.multiple_of(i * CH, CH), CH)
        o_ref[0, 0, sl, :] = (acc_s[sl, :] / l_s[sl, :]).astype(BF16)
        return 0

    lax.fori_loop(0, S // CH, store, 0)


def _attn_a(qkv):
    B, _, S, _ = qkv.shape
    npair = A_WIDTH // LANES
    pad = BLK * A_PATTERNS[-1][1]
    assert S % pad == 0 and S // pad >= 2
    blk = lambda off: pl.BlockSpec((1, 1, S, LANES), lambda b, h: (b, off + h, 0, 0))
    return pl.pallas_call(
        functools.partial(_attn_a_kernel, S=S, pad=pad),
        grid=(B, npair),
        in_specs=[blk(0), blk(npair), blk(2 * npair)],
        out_specs=blk(0),
        out_shape=jax.ShapeDtypeStruct((B, npair, S, LANES), BF16),
        scratch_shapes=[pltpu.VMEM((S, LANES), F32), pltpu.VMEM((S + pad, LANES), F32),
                        pltpu.VMEM((S + pad, LANES), F32), pltpu.VMEM((S, LANES), F32),
                        pltpu.VMEM((S, LANES), F32), pltpu.VMEM((S, LANES), F32)],
        compiler_params=_cparams("parallel", "parallel"),
    )(qkv, qkv, qkv)


def _attn_b_kernel(lam_ref, gain_ref, q_ref, k_ref, v_ref, o_ref, *, S, lam_init):
    TQ, TK = DIFF_TQ, DIFF_TK
    assert TQ == TK
    lp = lam_ref[...]
    lam = (jnp.exp(jnp.sum(lp[0:1] * lp[1:2], axis=-1, keepdims=True))
           - jnp.exp(jnp.sum(lp[2:3] * lp[3:4], axis=-1, keepdims=True)) + lam_init)
    h0 = _head_mask((TQ, LANES))
    row = lax.broadcasted_iota(jnp.int32, (2 * TQ, TK), 0)
    col = lax.broadcasted_iota(jnp.int32, (2 * TQ, TK), 1)
    causal = jnp.where(row >= TQ, row - TQ, row) >= col
    zero = jnp.zeros((), BF16)

    def q_block(qi, _):
        q = q_ref[0, 0, pl.ds(pl.multiple_of(qi * TQ, TQ), TQ), :]
        qs = jnp.concatenate([jnp.where(h0, q, zero), jnp.where(h0, zero, q)], axis=0)

        def kv_step(kj, carry, masked):
            m, l, acc = carry
            rows = pl.ds(pl.multiple_of(kj * TK, TK), TK)
            kb = k_ref[0, 0, rows, :]
            vb = v_ref[0, 0, rows, :]
            s = _dot_nt(qs, kb)
            if masked:
                s = jnp.where(causal, s, NEG)
            m_new = jnp.maximum(m, jnp.max(s, axis=-1, keepdims=True))
            alpha = jnp.exp(m - m_new)
            p = jnp.exp(s - m_new)
            l = alpha * l + jnp.sum(p, axis=-1, keepdims=True)
            acc = alpha * acc + _dot(p.astype(BF16), vb)
            return m_new, l, acc

        init = (jnp.full((2 * TQ, 1), NEG, F32), jnp.zeros((2 * TQ, 1), F32),
                jnp.zeros((2 * TQ, LANES), F32))
        carry = lax.fori_loop(0, qi, functools.partial(kv_step, masked=False), init)
        _, l, acc = kv_step(qi, carry, True)
        o = acc / l
        o = o[:TQ] - lam * o[TQ:]
        o = _rms(o, gain_ref[...]) * (1.0 - lam_init)
        o_ref[0, 0, pl.ds(pl.multiple_of(qi * TQ, TQ), TQ), :] = o.astype(BF16)
        return 0

    lax.fori_loop(0, S // TQ, q_block, 0)


def _attn_b(qkv, diff_lambda, subln, lam_init):
    B, _, S, _ = qkv.shape
    base = 3 * A_WIDTH // LANES
    blk = lambda off: pl.BlockSpec((1, 1, S, LANES), lambda b, h: (b, off + h, 0, 0))
    return pl.pallas_call(
        functools.partial(_attn_b_kernel, S=S, lam_init=lam_init),
        grid=(B, B_HEADS),
        in_specs=[_full_spec((4, HEAD_DIM)), _full_spec((1, 2 * HEAD_DIM)),
                  blk(base), blk(base + B_HEADS), blk(base + 2 * B_HEADS)],
        out_specs=blk(0),
        out_shape=jax.ShapeDtypeStruct((B, B_HEADS, S, LANES), BF16),
        compiler_params=_cparams("parallel", "parallel"),
    )(diff_lambda, subln.reshape(1, 2 * HEAD_DIM), qkv, qkv, qkv)


def _mix_mlp_kernel(*refs, n_act, final_norm):
    res_ref = refs[0]
    act_refs = refs[1:1 + n_act]
    wo_ref, g_ref, w1_ref, w2_ref = refs[1 + n_act:5 + n_act]
    gf_ref = refs[5 + n_act] if final_norm else None
    o_ref = refs[-1]
    act = jnp.concatenate([a[0, c] for a in act_refs for c in range(a.shape[1])], axis=-1)
    h = res_ref[0] + _dot(act, wo_ref[...])
    hn = _rms(h, g_ref[...]).astype(BF16)
    fc = D_MODEL
    acc = jnp.zeros_like(h)
    for f in range(D_FF // fc):
        a = _dot(hn, w1_ref[:, f * fc:(f + 1) * fc])
        a = jnp.square(jnp.maximum(a, 0.0)).astype(BF16)
        acc = acc + _dot(a, w2_ref[f * fc:(f + 1) * fc, :])
    h = h + acc
    if final_norm:
        h = _rms(h, gf_ref[...])
    o_ref[0] = h


def _mix_mlp(res, acts, w_o, g, w1, w2, g_final=None):
    B, S, C = res.shape
    tm = ROW_TILE
    final_norm = g_final is not None
    row = pl.BlockSpec((1, tm, C), lambda b, i: (b, i, 0))
    in_specs = [row]
    for a in acts:
        in_specs.append(pl.BlockSpec((1, a.shape[1], tm, LANES), lambda b, i: (b, 0, i, 0)))
    in_specs += [_full_spec(w_o.shape), _full_spec((1, C)), _full_spec(w1.shape), _full_spec(w2.shape)]
    args = [res, *acts, w_o.astype(BF16), g.reshape(1, C), w1.astype(BF16), w2.astype(BF16)]
    if final_norm:
        in_specs.append(_full_spec((1, C)))
        args.append(g_final.reshape(1, C))
    return pl.pallas_call(
        functools.partial(_mix_mlp_kernel, n_act=len(acts), final_norm=final_norm),
        grid=(B, S // tm),
        in_specs=in_specs,
        out_specs=row,
        out_shape=jax.ShapeDtypeStruct((B, S, C), F32),
        compiler_params=_cparams("parallel", "parallel"),
    )(*args)


def _rwkv_proj_kernel(x_ref, xp_ref, g_ref, mu_ref, wr_ref, wk_ref, wv_ref, w1_ref, w2_ref,
                      a1_ref, a2_ref, g1_ref, g2_ref, w0_ref, a0_ref, kk_ref, ka_ref,
                      r_out, lw_out, k_out, v_out, kk_out, a_out, g_out):
    i = pl.program_id(1)
    g = g_ref[...]
    hn = _rms(x_ref[0], g)
    tm = hn.shape[0]
    last = _rms(xp_ref[0], g)[7:8, :]
    last = jnp.where(i == 0, 0.0, last)
    row = lax.broadcasted_iota(jnp.int32, hn.shape, 0)
    prev = jnp.where(row == 0, last, pltpu.roll(hn, 1, 0))
    xx = prev - hn

    def mixed(j):
        return (hn + xx * mu_ref[j:j + 1, :]).astype(BF16)

    r = _dot(mixed(0), wr_ref[...])
    wl = jnp.tanh(_dot(mixed(1), w1_ref[...])).astype(BF16)
    u = w0_ref[...] + _dot(wl, w2_ref[...])
    w = -(jnp.maximum(-u, 0.0) + jnp.log(1.0 + jnp.exp(-jnp.abs(u)))) - 0.5
    lw = -jnp.exp(w)
    k = _dot(mixed(2), wk_ref[...])
    v = _dot(mixed(3), wv_ref[...])
    al = _dot(mixed(4), a1_ref[...]).astype(BF16)
    a = jax.nn.sigmoid(a0_ref[...] + _dot(al, a2_ref[...]))
    gl = jax.nn.sigmoid(_dot(mixed(5), g1_ref[...])).astype(BF16)
    gate = _dot(gl, g2_ref[...])
    kk = k * kk_ref[...]
    k = k * (1.0 + (a - 1.0) * ka_ref[...])

    ri = lax.broadcasted_iota(jnp.int32, (LANES, LANES), 0) // HEAD_DIM
    ci = lax.broadcasted_iota(jnp.int32, (LANES, LANES), 1) // HEAD_DIM
    seg = jnp.where(ri == ci, 1.0, 0.0).astype(BF16)
    for c in range(N_LANE_BLOCKS):
        sl = slice(c * LANES, (c + 1) * LANES)
        kc = kk[:, sl]
        hi, lo = _split2(kc * kc)
        ss = _dot(hi, seg) + _dot(lo, seg)
        kk_out[0, c] = (kc / jnp.maximum(jnp.sqrt(ss), 1e-12)).astype(BF16)
        r_out[0, c] = r[:, sl].astype(BF16)
        lw_out[0, c] = lw[:, sl]
        k_out[0, c] = k[:, sl].astype(BF16)
        v_out[0, c] = v[:, sl].astype(BF16)
        a_out[0, c] = a[:, sl].astype(BF16)
        g_out[0, c] = gate[:, sl].astype(BF16)


def _rwkv_proj(h, g, mu, w_r, w_k, w_v, w1, w2, a1, a2, g1, g2, w0, a0, k_k, k_a):
    B, S, C = h.shape
    tm = RWKV_ROW_TILE
    glr = g1.shape[1]
    g1p = jnp.pad(g1, ((0, 0), (0, GATE_LORA_PAD - glr))).astype(BF16)
    g2p = jnp.pad(g2, ((0, GATE_LORA_PAD - glr), (0, 0))).astype(BF16)
    vec = lambda t: t.reshape(1, C)
    args = [h, h, vec(g), mu, w_r.astype(BF16), w_k.astype(BF16), w_v.astype(BF16),
            w1.astype(BF16), w2.astype(BF16), a1.astype(BF16), a2.astype(BF16), g1p, g2p,
            vec(w0), vec(a0), vec(k_k), vec(k_a)]
    in_specs = [pl.BlockSpec((1, tm, C), lambda b, i: (b, i, 0)),
                pl.BlockSpec((1, 8, C), lambda b, i: (b, jnp.maximum(i * (tm // 8) - 1, 0), 0))]
    in_specs += [_full_spec(a.shape) for a in args[2:]]
    out_spec = pl.BlockSpec((1, N_LANE_BLOCKS, tm, LANES), lambda b, i: (b, 0, i, 0))
    shp = lambda dt: jax.ShapeDtypeStruct((B, N_LANE_BLOCKS, S, LANES), dt)
    return pl.pallas_call(
        _rwkv_proj_kernel,
        grid=(B, S // tm),
        in_specs=in_specs,
        out_specs=[out_spec] * 7,
        out_shape=[shp(BF16), shp(F32), shp(BF16), shp(BF16), shp(BF16), shp(BF16), shp(BF16)],
        compiler_params=_cparams("parallel", "parallel"),
    )(*args)


def _bmm(a, b):
    return lax.dot_general(a, b, (((2,), (1,)), ((0,), (0,))), preferred_element_type=F32)


def _bmm_nt(a, b):
    return lax.dot_general(a, b, (((2,), (2,)), ((0,), (0,))), preferred_element_type=F32)


def _bmm_tn(a, b):
    return lax.dot_general(a, b, (((1,), (1,)), ((0,), (0,))), preferred_element_type=F32)


def _bmm_f32(a, b, passes):
    if passes == 1:
        return _bmm(a.astype(BF16), b.astype(BF16))
    ah, al = _split2(a)
    bh, bl = _split2(b)
    return _bmm(ah, bh) + (_bmm(ah, bl) + _bmm(al, bh))


def _tri_inverse(n_mat, nilpotency, passes):
    m = n_mat.shape[-1]
    eye = jnp.where(lax.broadcasted_iota(jnp.int32, (m, m), 0) == lax.broadcasted_iota(jnp.int32, (m, m), 1),
                    1.0, 0.0)
    t = eye + n_mat
    pw = n_mat
    for _ in range(int(math.log2(nilpotency)) - 1):
        pw = _bmm_f32(pw, pw, passes)
        t = t + _bmm_f32(pw, t, passes)
    return t


def _rwkv_kernel(r_ref, lw_ref, k_ref, v_ref, kk_ref, a_ref, g_ref, rk_ref, lnw_ref, lnb_ref,
                 o_ref, z_s):
    @pl.when(pl.program_id(1) == 0)
    def _():
        z_s[...] = jnp.zeros_like(z_s)

    _, NP, CB, L, _ = r_ref.shape
    NB = NP * CB

    def chunks(ref):
        return ref[0].astype(F32).reshape(NB, L, LANES)

    r, lw, k, v, kk, a = (chunks(t) for t in (r_ref, lw_ref, k_ref, v_ref, kk_ref, a_ref))

    m0 = _head_mask((1, L, LANES))

    def stack(t):
        return jnp.concatenate([jnp.where(m0, t, 0.0), jnp.where(m0, 0.0, t)], axis=1)

    def unstack(t):
        return t[:, :L] + t[:, L:]

    ti = lax.broadcasted_iota(jnp.int32, (L, L), 0)
    si = lax.broadcasted_iota(jnp.int32, (L, L), 1)
    tri = jnp.broadcast_to(jnp.where(ti >= si, 1.0, 0.0).astype(BF16), (NB, L, L))
    t2 = lax.broadcasted_iota(jnp.int32, (2 * L, 2 * L), 0) % L
    s2 = lax.broadcasted_iota(jnp.int32, (2 * L, 2 * L), 1) % L
    strict = t2 > s2
    incl = t2 >= s2
    ri = lax.broadcasted_iota(jnp.int32, (LANES, LANES), 0)
    ci = lax.broadcasted_iota(jnp.int32, (LANES, LANES), 1)
    diag = ri == ci
    seg = jnp.where((ri // HEAD_DIM) == (ci // HEAD_DIM), 1.0, 0.0).astype(BF16)

    hi = lw.astype(BF16)
    r1 = lw - hi.astype(F32)
    mid = r1.astype(BF16)
    lo = (r1 - mid.astype(F32)).astype(BF16)
    cs = _bmm(tri, hi) + (_bmm(tri, mid) + _bmm(tri, lo))
    tot = cs[:, L - 1:L, :]
    p_tot = jnp.exp(tot)
    kka = kk * a
    a2 = stack(-(kk * jnp.exp(cs - lw)))
    r2 = stack(r * jnp.exp(cs))
    p_inv = jnp.exp(-cs)
    b2 = stack(kka * p_inv).astype(BF16)
    k2 = stack(k * p_inv).astype(BF16)
    p_rest = jnp.exp(tot - cs)
    bh2 = stack(kka * p_rest).astype(BF16)
    kh2 = stack(k * p_rest).astype(BF16)
    v2 = stack(v).astype(BF16)

    ar = jnp.concatenate([a2, r2], axis=1).astype(BF16)
    bk = jnp.concatenate([b2, k2], axis=1)
    m = _bmm_nt(ar, bk)
    H = 2 * L
    n_ab = jnp.where(strict, m[:, :H, :H], 0.0)
    a_ak = jnp.where(strict, m[:, :H, H:], 0.0).astype(BF16)
    a_rb = jnp.where(incl, m[:, H:, :H], 0.0).astype(BF16)
    a_rk = jnp.where(incl, m[:, H:, H:], 0.0).astype(BF16)
    t_inv = _tri_inverse(n_ab, L, RWKV_TRI_PASSES)
    akv = _bmm(a_ak, v2)
    x = jnp.concatenate([a2, akv], axis=2)
    wu = _bmm_f32(t_inv, x, RWKV_TRI_PASSES).astype(BF16)
    rbw = _bmm(a_rb, wu)
    rw = unstack(r2 + rbw[:, :, :LANES])
    yv = unstack(rbw[:, :, LANES:] + _bmm(a_rk, v2))
    gw = _bmm_tn(bh2, wu)
    g_mat = gw[:, :, :LANES] + jnp.where(diag, p_tot, 0.0)
    h_mat = gw[:, :, LANES:] + _bmm_tn(kh2, v2)

    rw = rw.reshape(NP, CB, L, LANES)
    yv = yv.reshape(NP, CB, L, LANES)
    g_mat = g_mat.reshape(NP, CB, LANES, LANES)
    h_mat = h_mat.reshape(NP, CB, LANES, LANES)
    z = z_s[...]
    ys = []
    for c in range(CB):
        ys.append(_bmm_f32(rw[:, c], z, RWKV_STATE_PASSES) + yv[:, c])
        z = _bmm_f32(g_mat[:, c], z, RWKV_STATE_PASSES) + h_mat[:, c]
    z_s[...] = z
    y = jnp.stack(ys, axis=1).reshape(NB * L, LANES)

    def head_sum(t):
        hi, lo = _split2(t)
        return _dot(hi, seg) + _dot(lo, seg)

    def rows(t):
        return t.reshape(NP, CB * L, LANES)

    inv_n = 1.0 / HEAD_DIM
    yc = y - head_sum(y) * inv_n
    var = head_sum(yc * yc) * inv_n
    yn = rows(yc * lax.rsqrt(var + GN_EPS)) * lnw_ref[...] + lnb_ref[...]
    rk = rows(r * k) * rk_ref[...]
    bonus = rows(head_sum(rk.reshape(NB * L, LANES))) * rows(v)
    gate = g_ref[0].astype(F32).reshape(NP, CB * L, LANES)
    o_ref[0] = ((yn + bonus) * gate).astype(BF16).reshape(NP, CB, L, LANES)


def _rwkv_mix(r, lw, k, v, kk, a, gate, r_k, ln_w, ln_b):
    B, NP, S, _ = r.shape
    L, CB = RWKV_CHUNK, RWKV_CHUNKS_PER_STEP
    nc = S // L
    chunked = lambda t: t.reshape(B, NP, nc, L, LANES)
    blk = pl.BlockSpec((1, NP, CB, L, LANES), lambda b, c: (b, 0, c, 0, 0))
    par = _full_spec((NP, 1, LANES))
    pv = lambda t: t.reshape(NP, 1, LANES)
    out = pl.pallas_call(
        _rwkv_kernel,
        grid=(B, nc // CB),
        in_specs=[blk] * 7 + [par] * 3,
        out_specs=blk,
        out_shape=jax.ShapeDtypeStruct((B, NP, nc, L, LANES), BF16),
        scratch_shapes=[pltpu.VMEM((NP, LANES, LANES), F32)],
        compiler_params=_cparams("parallel", "arbitrary"),
    )(*(chunked(t) for t in (r, lw, k, v, kk, a, gate)), pv(r_k), pv(ln_w), pv(ln_b))
    return out.reshape(B, NP, S, LANES)


def kernel(x, norm_mix, norm_mlp, norm_final, attn_w_in, attn_w_out, diff_lambda, diff_subln, rwkv_mu, rwkv_w_r, rwkv_w_k, rwkv_w_v, rwkv_w_o, rwkv_w0, rwkv_w1, rwkv_w2, rwkv_a0, rwkv_a1, rwkv_a2, rwkv_g1, rwkv_g2, rwkv_k_k, rwkv_k_a, rwkv_r_k, rwkv_ln_w, rwkv_ln_b, mlp_w1, mlp_w2):
    lam_init = 0.8 - 0.6 * math.exp(-0.3 * 0)
    qkv = _qkv_proj(x, norm_mix[0], attn_w_in[0])
    oa = _attn_a(qkv)
    ob = _attn_b(qkv, diff_lambda[0], diff_subln[0], lam_init)
    h = _mix_mlp(x, [oa, ob], attn_w_out[0], norm_mlp[0], mlp_w1[0], mlp_w2[0])
    r, lw, k, v, kk, a, gate = _rwkv_proj(
        h, norm_mix[1], rwkv_mu[0], rwkv_w_r[0], rwkv_w_k[0], rwkv_w_v[0], rwkv_w1[0], rwkv_w2[0],
        rwkv_a1[0], rwkv_a2[0], rwkv_g1[0], rwkv_g2[0], rwkv_w0[0], rwkv_a0[0], rwkv_k_k[0], rwkv_k_a[0])
    mix = _rwkv_mix(r, lw, k, v, kk, a, gate, rwkv_r_k[0], rwkv_ln_w[0], rwkv_ln_b[0])
    return _mix_mlp(h, [mix], rwkv_w_o[0], norm_mlp[1], mlp_w1[1], mlp_w2[1], g_final=norm_final)
```

```python
import functools
import math

import jax
import jax.numpy as jnp
from jax import lax
from jax.experimental import pallas as pl
from jax.experimental.pallas import tpu as pltpu

F32 = jnp.float32
BF16 = jnp.bfloat16

D_MODEL = 1024
HEAD_DIM = 64
ROT_DIM = HEAD_DIM // 4
ROPE_THETA = 500000.0
BLK = 128
A_PATTERNS = ((128, 1), (512, 4), (2048, 16))
A_WIDTH = 512
B_HEADS = 4
B_WIDTH = 512
QKV_COLS = 3072
GN_EPS = 64e-5
D_FF = 4 * D_MODEL
EPS = 1e-5
GATE_LORA_PAD = 256

LANES = 128
N_LANE_BLOCKS = D_MODEL // LANES
NEG = -1e30
VMEM_LIMIT = 56 * 1024 * 1024

ROW_TILE = 512
RWKV_ROW_TILE = 256
RWKV_CHUNK = 64
RWKV_CHUNKS_PER_STEP = 2
RWKV_TRI_PASSES = 1
RWKV_STATE_PASSES = 3
DIFF_TQ = 512
DIFF_TK = 512
ATTN_A_UNITS_PER_STEP = 4


def _cparams(*semantics):
    return pltpu.CompilerParams(dimension_semantics=semantics, vmem_limit_bytes=VMEM_LIMIT)


def _dot(a, b):
    return jnp.dot(a, b, preferred_element_type=F32)


def _dot_nt(a, b):
    return lax.dot_general(a, b, (((1,), (1,)), ((), ())), preferred_element_type=F32)


def _split2(x):
    hi = x.astype(BF16)
    lo = (x - hi.astype(F32)).astype(BF16)
    return hi, lo


def _dot3(a, b):
    ah, al = _split2(a)
    bh, bl = _split2(b)
    return _dot(ah, bh) + (_dot(ah, bl) + _dot(al, bh))


def _rms(x, g):
    return x * lax.rsqrt(jnp.mean(x * x, axis=-1, keepdims=True) + EPS) * g


def _head_mask(shape):
    return lax.broadcasted_iota(jnp.int32, shape, len(shape) - 1) < HEAD_DIM


def _full_spec(shape):
    nd = len(shape)
    return pl.BlockSpec(shape, lambda *_: (0,) * nd)


def _rope_tables(S):
    half = ROT_DIM // 2
    inv_freq = ROPE_THETA ** (-jnp.arange(half, dtype=F32) / half)
    ang = jnp.arange(S, dtype=F32)[:, None] * inv_freq[None, :]
    d = jnp.arange(LANES) % HEAD_DIM
    ang_l = ang[:, d % half]
    lo = (d < half)[None, :]
    hi = ((d >= half) & (d < ROT_DIM))[None, :]
    cos = jnp.where(lo | hi, jnp.cos(ang_l), 1.0)
    sin_up = jnp.where(hi, jnp.sin(ang_l), 0.0)
    sin_dn = jnp.where(lo, -jnp.sin(ang_l), 0.0)
    return cos, sin_up, sin_dn


_QKV_GROUPS = ((True, HEAD_DIM ** -0.5), (True, 1.0), (False, 1.0),
               (True, HEAD_DIM ** -0.5), (True, 1.0), (False, 1.0))


def _qkv_kernel(x_ref, g_ref, w_ref, cos_ref, su_ref, sd_ref, o_ref, vt_ref):
    hn = _rms(x_ref[0], g_ref[...]).astype(BF16)
    cos, su, sd = cos_ref[...], su_ref[...], sd_ref[...]
    half = ROT_DIM // 2
    tk = vt_ref.shape[-1]
    for j, (rot, scale) in enumerate(_QKV_GROUPS):
        y = _dot(hn, w_ref[:, j * 512:(j + 1) * 512])
        for c in range(4):
            t = y[:, c * LANES:(c + 1) * LANES]
            if rot:
                t = t * cos + pltpu.roll(t, half, 1) * su + pltpu.roll(t, LANES - half, 1) * sd
            if scale != 1.0:
                t = t * scale
            if j < len(_QKV_GROUPS) - 1:
                o_ref[0, 4 * j + c] = t.astype(BF16)
            else:
                tt = t.T.astype(BF16)
                for i in range(vt_ref.shape[2]):
                    vt_ref[0, c, i] = tt[:, i * tk:(i + 1) * tk]


def _qkv_proj(x, g, w_in):
    B, S, C = x.shape
    tm = ROW_TILE
    cos, su, sd = _rope_tables(S)
    nblk = QKV_COLS // LANES - B_HEADS
    tab = pl.BlockSpec((tm, LANES), lambda b, i: (i, 0))
    return pl.pallas_call(
        _qkv_kernel,
        grid=(B, S // tm),
        in_specs=[pl.BlockSpec((1, tm, C), lambda b, i: (b, i, 0)),
                  _full_spec((1, C)), _full_spec((C, QKV_COLS)), tab, tab, tab],
        out_specs=[pl.BlockSpec((1, nblk, tm, LANES), lambda b, i: (b, 0, i, 0)),
                   pl.BlockSpec((1, B_HEADS, tm // DIFF_TK, LANES, DIFF_TK), lambda b, i: (b, 0, i, 0, 0))],
        out_shape=[jax.ShapeDtypeStruct((B, nblk, S, LANES), BF16),
                   jax.ShapeDtypeStruct((B, B_HEADS, S // DIFF_TK, LANES, DIFF_TK), BF16)],
        compiler_params=_cparams("parallel", "parallel"),
    )(x, g.reshape(1, C), w_in.astype(BF16), cos, su, sd)


def _attn_a_kernel(q_ref, k_ref, v_ref, o_ref, qf, kf, vf, acc0_s, acc1_s, m0_s, m1_s, *, S, pad):
    CH = 512

    def load(i, _):
        sl = pl.ds(pl.multiple_of(i * CH, CH), CH)
        dst = pl.ds(pl.multiple_of(pad + i * CH, CH), CH)
        qf[sl, :] = q_ref[0, 0, sl, :].astype(F32)
        kf[dst, :] = k_ref[0, 0, sl, :].astype(F32)
        vf[dst, :] = v_ref[0, 0, sl, :].astype(F32)
        return 0

    lax.fori_loop(0, S // CH, load, 0)
    kf[pl.ds(0, pad), :] = jnp.zeros((pad, LANES), F32)
    vf[pl.ds(0, pad), :] = jnp.zeros((pad, LANES), F32)

    h0 = _head_mask((BLK, LANES))
    qi = lax.broadcasted_iota(jnp.int32, (BLK, 2 * BLK), 0)
    kc = lax.broadcasted_iota(jnp.int32, (BLK, 2 * BLK), 1)
    dist = BLK + qi - kc
    band = (dist >= 0) & (dist <= BLK)
    bias = jnp.where(band, 0.0, NEG)
    bias_first = jnp.where(band & (kc >= BLK), 0.0, NEG)

    h0k = _head_mask((2 * BLK, LANES))
    heads = ((h0, h0k, acc0_s, m0_s), (jnp.logical_not(h0), jnp.logical_not(h0k), acc1_s, m1_s))
    U = ATTN_A_UNITS_PER_STEP

    for p, (window, d) in enumerate(A_PATTERNS):
        assert window // d == BLK
        nb = S // (d * BLK)
        assert (d * nb) % U == 0

        def group(gi, _, p=p, d=d):
            loaded = []
            for i in range(U):
                u = gi * U + i
                r = u % d
                n = u // d
                if d > 1:
                    rows_q = pl.ds(r + n * (BLK * d), BLK, stride=d)
                    rows_k = pl.ds(pad + r + (n - 1) * (BLK * d), 2 * BLK, stride=d)
                else:
                    rows_q = pl.ds(pl.multiple_of(n * BLK, BLK), BLK)
                    rows_k = pl.ds(pl.multiple_of(pad + (n - 1) * BLK, BLK), 2 * BLK)
                old = [(acc[rows_q, :], m[rows_q, :]) for _, _, acc, m in heads] if p > 0 else None
                loaded.append((n, rows_q, qf[rows_q, :], kf[rows_k, :], vf[rows_k, :], old))
            results = []
            for n, rows_q, q, k, v, old in loaded:
                kb = k.astype(BF16)
                bias_n = jnp.where(n == 0, bias_first, bias)
                res = []
                for h, (mq, mk, _, _) in enumerate(heads):
                    qh = jnp.where(mq, q, 0.0).astype(BF16)
                    vh = jnp.where(mk, v, 1.0).astype(BF16)
                    s = _dot_nt(qh, kb) + bias_n
                    m_new = jnp.max(s, axis=-1, keepdims=True)
                    pv = _dot(jnp.exp(s - m_new).astype(BF16), vh)
                    if p == 0:
                        res.append((pv, jnp.broadcast_to(m_new, (BLK, LANES))))
                    else:
                        a_old, m_old = old[h]
                        m_tot = jnp.maximum(m_old, m_new)
                        res.append((a_old * jnp.exp(m_old - m_tot) + pv * jnp.exp(m_new - m_tot), m_tot))
                results.append((rows_q, res))
            for rows_q, res in results:
                for (_, _, acc, m), (a_val, m_val) in zip(heads, res):
                    acc[rows_q, :] = a_val
                    m[rows_q, :] = m_val
            return 0

        lax.fori_loop(0, d * nb // U, group, 0)

    def store(i, _):
        sl = pl.ds(pl.multiple_of(i * BLK, BLK), BLK)
        a0 = acc0_s[sl, :]
        a1 = acc1_s[sl, :]
        o = jnp.where(h0, a0 / pltpu.roll(a0, HEAD_DIM, 1), a1 / pltpu.roll(a1, HEAD_DIM, 1))
        o_ref[0, 0, sl, :] = o.astype(BF16)
        return 0

    lax.fori_loop(0, S // BLK, store, 0)


def _attn_a(qkv):
    B, _, S, _ = qkv.shape
    npair = A_WIDTH // LANES
    pad = BLK * A_PATTERNS[-1][1]
    assert S % pad == 0 and S // pad >= 2
    blk = lambda off: pl.BlockSpec((1, 1, S, LANES), lambda b, h: (b, off + h, 0, 0))
    return pl.pallas_call(
        functools.partial(_attn_a_kernel, S=S, pad=pad),
        grid=(B, npair),
        in_specs=[blk(0), blk(npair), blk(2 * npair)],
        out_specs=blk(0),
        out_shape=jax.ShapeDtypeStruct((B, npair, S, LANES), BF16),
        scratch_shapes=[pltpu.VMEM((S, LANES), F32), pltpu.VMEM((S + pad, LANES), F32),
                        pltpu.VMEM((S + pad, LANES), F32)] + [pltpu.VMEM((S, LANES), F32)] * 4,
        compiler_params=_cparams("parallel", "parallel"),
    )(qkv, qkv, qkv)


def _attn_b_kernel(lam_ref, gain_ref, q_ref, k_ref, vt_ref, o_ref, *, S, lam_init):
    TQ, TK = DIFF_TQ, DIFF_TK
    assert TQ == TK
    lp = lam_ref[...]
    lam = (jnp.exp(jnp.sum(lp[0:1] * lp[1:2], axis=-1, keepdims=True))
           - jnp.exp(jnp.sum(lp[2:3] * lp[3:4], axis=-1, keepdims=True)) + lam_init)
    h0 = _head_mask((TQ, LANES))
    causal = (lax.broadcasted_iota(jnp.int32, (TK, TQ), 1) >= lax.broadcasted_iota(jnp.int32, (TK, TQ), 0))
    zero = jnp.zeros((), BF16)

    def q_block(qi, _):
        q = q_ref[0, 0, pl.ds(pl.multiple_of(qi * TQ, TQ), TQ), :]
        q_maps = (jnp.where(h0, q, zero), jnp.where(h0, zero, q))

        def kv_step(kj, carry, masked):
            kb = k_ref[0, 0, pl.ds(pl.multiple_of(kj * TK, TK), TK), :]
            vt = vt_ref[0, 0, kj]
            out = []
            for qm, (m, l, acc) in zip(q_maps, carry):
                st = _dot_nt(kb, qm)
                if masked:
                    st = jnp.where(causal, st, NEG)
                m_new = jnp.maximum(m, jnp.max(st, axis=0, keepdims=True))
                alpha = jnp.exp(m - m_new)
                pt = jnp.exp(st - m_new)
                l = alpha * l + jnp.sum(pt, axis=0, keepdims=True)
                acc = alpha * acc + _dot(vt, pt.astype(BF16))
                out.append((m_new, l, acc))
            return tuple(out)

        init = tuple((jnp.full((1, TQ), NEG, F32), jnp.zeros((1, TQ), F32), jnp.zeros((LANES, TQ), F32))
                     for _ in q_maps)
        carry = lax.fori_loop(0, qi, functools.partial(kv_step, masked=False), init)
        (_, l0, acc0), (_, l1, acc1) = kv_step(qi, carry, True)
        o = acc0 / l0 - lam * (acc1 / l1)
        o = o * lax.rsqrt(jnp.mean(o * o, axis=0, keepdims=True) + EPS) * (gain_ref[...] * (1.0 - lam_init))
        o_ref[0, 0, pl.ds(pl.multiple_of(qi * TQ, TQ), TQ), :] = o.T.astype(BF16)
        return 0

    lax.fori_loop(0, S // TQ, q_block, 0)


def _attn_b(qkv, vt, diff_lambda, subln, lam_init):
    B, _, S, _ = qkv.shape
    base = 3 * A_WIDTH // LANES
    blk = lambda off: pl.BlockSpec((1, 1, S, LANES), lambda b, h: (b, off + h, 0, 0))
    return pl.pallas_call(
        functools.partial(_attn_b_kernel, S=S, lam_init=lam_init),
        grid=(B, B_HEADS),
        in_specs=[_full_spec((4, HEAD_DIM)), _full_spec((2 * HEAD_DIM, 1)),
                  blk(base), blk(base + B_HEADS),
                  pl.BlockSpec((1, 1, S // DIFF_TK, LANES, DIFF_TK), lambda b, h: (b, h, 0, 0, 0))],
        out_specs=blk(0),
        out_shape=jax.ShapeDtypeStruct((B, B_HEADS, S, LANES), BF16),
        compiler_params=_cparams("parallel", "parallel"),
    )(diff_lambda, subln.reshape(2 * HEAD_DIM, 1), qkv, qkv, vt)


def _mix_mlp_kernel(*refs, n_act, final_norm):
    res_ref = refs[0]
    act_refs = refs[1:1 + n_act]
    wo_ref, g_ref, w1_ref, w2_ref = refs[1 + n_act:5 + n_act]
    gf_ref = refs[5 + n_act] if final_norm else None
    o_ref = refs[-1]
    act = jnp.concatenate([a[0, c] for a in act_refs for c in range(a.shape[1])], axis=-1)
    h = res_ref[0] + _dot(act, wo_ref[...])
    hn = _rms(h, g_ref[...]).astype(BF16)
    fc = D_MODEL
    acc = jnp.zeros_like(h)
    for f in range(D_FF // fc):
        a = _dot(hn, w1_ref[:, f * fc:(f + 1) * fc])
        a = jnp.square(jnp.maximum(a, 0.0)).astype(BF16)
        acc = acc + _dot(a, w2_ref[f * fc:(f + 1) * fc, :])
    h = h + acc
    if final_norm:
        h = _rms(h, gf_ref[...])
    o_ref[0] = h


def _mix_mlp(res, acts, w_o, g, w1, w2, g_final=None):
    B, S, C = res.shape
    tm = ROW_TILE
    final_norm = g_final is not None
    row = pl.BlockSpec((1, tm, C), lambda b, i: (b, i, 0))
    in_specs = [row]
    for a in acts:
        in_specs.append(pl.BlockSpec((1, a.shape[1], tm, LANES), lambda b, i: (b, 0, i, 0)))
    in_specs += [_full_spec(w_o.shape), _full_spec((1, C)), _full_spec(w1.shape), _full_spec(w2.shape)]
    args = [res, *acts, w_o.astype(BF16), g.reshape(1, C), w1.astype(BF16), w2.astype(BF16)]
    if final_norm:
        in_specs.append(_full_spec((1, C)))
        args.append(g_final.reshape(1, C))
    return pl.pallas_call(
        functools.partial(_mix_mlp_kernel, n_act=len(acts), final_norm=final_norm),
        grid=(B, S // tm),
        in_specs=in_specs,
        out_specs=row,
        out_shape=jax.ShapeDtypeStruct((B, S, C), F32),
        compiler_params=_cparams("parallel", "parallel"),
    )(*args)


def _rwkv_proj_kernel(x_ref, xp_ref, g_ref, mu_ref, wr_ref, wk_ref, wv_ref, w1_ref, w2_ref,
                      a1_ref, a2_ref, g1_ref, g2_ref, w0_ref, a0_ref, kk_ref, ka_ref,
                      r_out, lw_out, k_out, v_out, kk_out, a_out, g_out):
    i = pl.program_id(1)
    g = g_ref[...]
    hn = _rms(x_ref[0], g)
    tm = hn.shape[0]
    last = _rms(xp_ref[0], g)[7:8, :]
    last = jnp.where(i == 0, 0.0, last)
    row = lax.broadcasted_iota(jnp.int32, hn.shape, 0)
    prev = jnp.where(row == 0, last, pltpu.roll(hn, 1, 0))
    xx = prev - hn

    def mixed(j):
        return (hn + xx * mu_ref[j:j + 1, :]).astype(BF16)

    r = _dot(mixed(0), wr_ref[...])
    wl = jnp.tanh(_dot(mixed(1), w1_ref[...])).astype(BF16)
    u = w0_ref[...] + _dot(wl, w2_ref[...])
    w = -(jnp.maximum(-u, 0.0) + jnp.log(1.0 + jnp.exp(-jnp.abs(u)))) - 0.5
    lw = -jnp.exp(w)
    k = _dot(mixed(2), wk_ref[...])
    v = _dot(mixed(3), wv_ref[...])
    al = _dot(mixed(4), a1_ref[...]).astype(BF16)
    a = jax.nn.sigmoid(a0_ref[...] + _dot(al, a2_ref[...]))
    gl = jax.nn.sigmoid(_dot(mixed(5), g1_ref[...])).astype(BF16)
    gate = _dot(gl, g2_ref[...])
    kk = k * kk_ref[...]
    k = k * (1.0 + (a - 1.0) * ka_ref[...])

    ri = lax.broadcasted_iota(jnp.int32, (LANES, LANES), 0) // HEAD_DIM
    ci = lax.broadcasted_iota(jnp.int32, (LANES, LANES), 1) // HEAD_DIM
    seg = jnp.where(ri == ci, 1.0, 0.0).astype(BF16)
    for c in range(N_LANE_BLOCKS):
        sl = slice(c * LANES, (c + 1) * LANES)
        kc = kk[:, sl]
        hi, lo = _split2(kc * kc)
        ss = _dot(hi, seg) + _dot(lo, seg)
        kk_out[0, c] = (kc / jnp.maximum(jnp.sqrt(ss), 1e-12)).astype(BF16)
        r_out[0, c] = r[:, sl].astype(BF16)
        lw_out[0, c] = lw[:, sl]
        k_out[0, c] = k[:, sl].astype(BF16)
        v_out[0, c] = v[:, sl].astype(BF16)
        a_out[0, c] = a[:, sl].astype(BF16)
        g_out[0, c] = gate[:, sl].astype(BF16)


def _rwkv_proj(h, g, mu, w_r, w_k, w_v, w1, w2, a1, a2, g1, g2, w0, a0, k_k, k_a):
    B, S, C = h.shape
    tm = RWKV_ROW_TILE
    glr = g1.shape[1]
    g1p = jnp.pad(g1, ((0, 0), (0, GATE_LORA_PAD - glr))).astype(BF16)
    g2p = jnp.pad(g2, ((0, GATE_LORA_PAD - glr), (0, 0))).astype(BF16)
    vec = lambda t: t.reshape(1, C)
    args = [h, h, vec(g), mu, w_r.astype(BF16), w_k.astype(BF16), w_v.astype(BF16),
            w1.astype(BF16), w2.astype(BF16), a1.astype(BF16), a2.astype(BF16), g1p, g2p,
            vec(w0), vec(a0), vec(k_k), vec(k_a)]
    in_specs = [pl.BlockSpec((1, tm, C), lambda b, i: (b, i, 0)),
                pl.BlockSpec((1, 8, C), lambda b, i: (b, jnp.maximum(i * (tm // 8) - 1, 0), 0))]
    in_specs += [_full_spec(a.shape) for a in args[2:]]
    out_spec = pl.BlockSpec((1, N_LANE_BLOCKS, tm, LANES), lambda b, i: (b, 0, i, 0))
    shp = lambda dt: jax.ShapeDtypeStruct((B, N_LANE_BLOCKS, S, LANES), dt)
    return pl.pallas_call(
        _rwkv_proj_kernel,
        grid=(B, S // tm),
        in_specs=in_specs,
        out_specs=[out_spec] * 7,
        out_shape=[shp(BF16), shp(F32), shp(BF16), shp(BF16), shp(BF16), shp(BF16), shp(BF16)],
        compiler_params=_cparams("parallel", "parallel"),
    )(*args)


def _bmm(a, b):
    return lax.dot_general(a, b, (((2,), (1,)), ((0,), (0,))), preferred_element_type=F32)


def _bmm_nt(a, b):
    return lax.dot_general(a, b, (((2,), (2,)), ((0,), (0,))), preferred_element_type=F32)


def _bmm_tn(a, b):
    return lax.dot_general(a, b, (((1,), (1,)), ((0,), (0,))), preferred_element_type=F32)


def _bmm_f32(a, b, passes):
    if passes == 1:
        return _bmm(a.astype(BF16), b.astype(BF16))
    ah, al = _split2(a)
    bh, bl = _split2(b)
    return _bmm(ah, bh) + (_bmm(ah, bl) + _bmm(al, bh))


def _tri_inverse(n_mat, nilpotency, passes):
    m = n_mat.shape[-1]
    eye = jnp.where(lax.broadcasted_iota(jnp.int32, (m, m), 0) == lax.broadcasted_iota(jnp.int32, (m, m), 1),
                    1.0, 0.0)
    t = eye + n_mat
    pw = n_mat
    for _ in range(int(math.log2(nilpotency)) - 1):
        pw = _bmm_f32(pw, pw, passes)
        t = t + _bmm_f32(pw, t, passes)
    return t


def _rwkv_kernel(r_ref, lw_ref, k_ref, v_ref, kk_ref, a_ref, g_ref, rk_ref, lnw_ref, lnb_ref,
                 o_ref, z_s):
    @pl.when(pl.program_id(1) == 0)
    def _():
        z_s[...] = jnp.zeros_like(z_s)

    _, NP, CB, L, _ = r_ref.shape
    NB = NP * CB

    def chunks(ref):
        return ref[0].astype(F32).reshape(NB, L, LANES)

    r, lw, k, v, kk, a = (chunks(t) for t in (r_ref, lw_ref, k_ref, v_ref, kk_ref, a_ref))

    m0 = _head_mask((1, L, LANES))

    def stack(t):
        return jnp.concatenate([jnp.where(m0, t, 0.0), jnp.where(m0, 0.0, t)], axis=1)

    def unstack(t):
        return t[:, :L] + t[:, L:]

    ti = lax.broadcasted_iota(jnp.int32, (L, L), 0)
    si = lax.broadcasted_iota(jnp.int32, (L, L), 1)
    tri = jnp.broadcast_to(jnp.where(ti >= si, 1.0, 0.0).astype(BF16), (NB, L, L))
    t2 = lax.broadcasted_iota(jnp.int32, (2 * L, 2 * L), 0) % L
    s2 = lax.broadcasted_iota(jnp.int32, (2 * L, 2 * L), 1) % L
    strict = t2 > s2
    incl = t2 >= s2
    ri = lax.broadcasted_iota(jnp.int32, (LANES, LANES), 0)
    ci = lax.broadcasted_iota(jnp.int32, (LANES, LANES), 1)
    diag = ri == ci
    seg = jnp.where((ri // HEAD_DIM) == (ci // HEAD_DIM), 1.0, 0.0).astype(BF16)

    hi = lw.astype(BF16)
    r1 = lw - hi.astype(F32)
    mid = r1.astype(BF16)
    lo = (r1 - mid.astype(F32)).astype(BF16)
    cs = _bmm(tri, hi) + (_bmm(tri, mid) + _bmm(tri, lo))
    tot = cs[:, L - 1:L, :]
    p_tot = jnp.exp(tot)
    kka = kk * a
    a2 = stack(-(kk * jnp.exp(cs - lw)))
    r2 = stack(r * jnp.exp(cs))
    p_inv = jnp.exp(-cs)
    b2 = stack(kka * p_inv).astype(BF16)
    k2 = stack(k * p_inv).astype(BF16)
    p_rest = jnp.exp(tot - cs)
    bh2 = stack(kka * p_rest).astype(BF16)
    kh2 = stack(k * p_rest).astype(BF16)
    v2 = stack(v).astype(BF16)

    ar = jnp.concatenate([a2, r2], axis=1).astype(BF16)
    bk = jnp.concatenate([b2, k2], axis=1)
    m = _bmm_nt(ar, bk)
    H = 2 * L
    n_ab = jnp.where(strict, m[:, :H, :H], 0.0)
    a_ak = jnp.where(strict, m[:, :H, H:], 0.0).astype(BF16)
    a_rb = jnp.where(incl, m[:, H:, :H], 0.0).astype(BF16)
    a_rk = jnp.where(incl, m[:, H:, H:], 0.0).astype(BF16)
    t_inv = _tri_inverse(n_ab, L, RWKV_TRI_PASSES)
    akv = _bmm(a_ak, v2)
    x = jnp.concatenate([a2, akv], axis=2)
    wu = _bmm_f32(t_inv, x, RWKV_TRI_PASSES).astype(BF16)
    rbw = _bmm(a_rb, wu)
    rw = unstack(r2 + rbw[:, :, :LANES])
    yv = unstack(rbw[:, :, LANES:] + _bmm(a_rk, v2))
    gw = _bmm_tn(bh2, wu)
    g_mat = gw[:, :, :LANES] + jnp.where(diag, p_tot, 0.0)
    h_mat = gw[:, :, LANES:] + _bmm_tn(kh2, v2)

    rw = rw.reshape(NP, CB, L, LANES)
    yv = yv.reshape(NP, CB, L, LANES)
    g_mat = g_mat.reshape(NP, CB, LANES, LANES)
    h_mat = h_mat.reshape(NP, CB, LANES, LANES)
    z = z_s[...]
    ys = []
    for c in range(CB):
        ys.append(_bmm_f32(rw[:, c], z, RWKV_STATE_PASSES) + yv[:, c])
        z = _bmm_f32(g_mat[:, c], z, RWKV_STATE_PASSES) + h_mat[:, c]
    z_s[...] = z
    y = jnp.stack(ys, axis=1).reshape(NB * L, LANES)

    def head_sum(t):
        hi, lo = _split2(t)
        return _dot(hi, seg) + _dot(lo, seg)

    def rows(t):
        return t.reshape(NP, CB * L, LANES)

    inv_n = 1.0 / HEAD_DIM
    yc = y - head_sum(y) * inv_n
    var = head_sum(yc * yc) * inv_n
    yn = rows(yc * lax.rsqrt(var + GN_EPS)) * lnw_ref[...] + lnb_ref[...]
    rk = rows(r * k) * rk_ref[...]
    bonus = rows(head_sum(rk.reshape(NB * L, LANES))) * rows(v)
    gate = g_ref[0].astype(F32).reshape(NP, CB * L, LANES)
    o_ref[0] = ((yn + bonus) * gate).astype(BF16).reshape(NP, CB, L, LANES)


def _rwkv_mix(r, lw, k, v, kk, a, gate, r_k, ln_w, ln_b):
    B, NP, S, _ = r.shape
    L, CB = RWKV_CHUNK, RWKV_CHUNKS_PER_STEP
    nc = S // L
    chunked = lambda t: t.reshape(B, NP, nc, L, LANES)
    blk = pl.BlockSpec((1, NP, CB, L, LANES), lambda b, c: (b, 0, c, 0, 0))
    par = _full_spec((NP, 1, LANES))
    pv = lambda t: t.reshape(NP, 1, LANES)
    out = pl.pallas_call(
        _rwkv_kernel,
        grid=(B, nc // CB),
        in_specs=[blk] * 7 + [par] * 3,
        out_specs=blk,
        out_shape=jax.ShapeDtypeStruct((B, NP, nc, L, LANES), BF16),
        scratch_shapes=[pltpu.VMEM((NP, LANES, LANES), F32)],
        compiler_params=_cparams("parallel", "arbitrary"),
    )(*(chunked(t) for t in (r, lw, k, v, kk, a, gate)), pv(r_k), pv(ln_w), pv(ln_b))
    return out.reshape(B, NP, S, LANES)


def kernel(x, norm_mix, norm_mlp, norm_final, attn_w_in, attn_w_out, diff_lambda, diff_subln, rwkv_mu, rwkv_w_r, rwkv_w_k, rwkv_w_v, rwkv_w_o, rwkv_w0, rwkv_w1, rwkv_w2, rwkv_a0, rwkv_a1, rwkv_a2, rwkv_g1, rwkv_g2, rwkv_k_k, rwkv_k_a, rwkv_r_k, rwkv_ln_w, rwkv_ln_b, mlp_w1, mlp_w2):
    lam_init = 0.8 - 0.6 * math.exp(-0.3 * 0)
    qkv, vt = _qkv_proj(x, norm_mix[0], attn_w_in[0])
    oa = _attn_a(qkv)
    ob = _attn_b(qkv, vt, diff_lambda[0], diff_subln[0], lam_init)
    h = _mix_mlp(x, [oa, ob], attn_w_out[0], norm_mlp[0], mlp_w1[0], mlp_w2[0])
    r, lw, k, v, kk, a, gate = _rwkv_proj(
        h, norm_mix[1], rwkv_mu[0], rwkv_w_r[0], rwkv_w_k[0], rwkv_w_v[0], rwkv_w1[0], rwkv_w2[0],
        rwkv_a1[0], rwkv_a2[0], rwkv_g1[0], rwkv_g2[0], rwkv_w0[0], rwkv_a0[0], rwkv_k_k[0], rwkv_k_a[0])
    mix = _rwkv_mix(r, lw, k, v, kk, a, gate, rwkv_r_k[0], rwkv_ln_w[0], rwkv_ln_b[0])
    return _mix_mlp(h, [mix], rwkv_w_o[0], norm_mlp[1], mlp_w1[1], mlp_w2[1], g_final=norm_final)
```

```python
import functools
import math

import jax
import jax.numpy as jnp
from jax import lax
from jax.experimental import pallas as pl
from jax.experimental.pallas import tpu as pltpu

F32 = jnp.float32
BF16 = jnp.bfloat16

D_MODEL = 1024
HEAD_DIM = 64
ROT_DIM = HEAD_DIM // 4
ROPE_THETA = 500000.0
BLK = 128
A_PATTERNS = ((128, 1), (512, 4), (2048, 16))
A_WIDTH = 512
B_HEADS = 4
B_WIDTH = 512
QKV_COLS = 3072
GN_EPS = 64e-5
D_FF = 4 * D_MODEL
EPS = 1e-5
GATE_LORA_PAD = 256

LANES = 128
N_LANE_BLOCKS = D_MODEL // LANES
NEG = -1e30
VMEM_LIMIT = 56 * 1024 * 1024

ROW_TILE = 512
RWKV_ROW_TILE = 256
RWKV_CHUNK = 64
RWKV_CHUNKS_PER_STEP = 2
DIFF_TQ = 512
DIFF_TK = 512
ATTN_A_UNITS_PER_STEP = 4


def _cparams(*semantics):
    return pltpu.CompilerParams(dimension_semantics=semantics, vmem_limit_bytes=VMEM_LIMIT)


def _dot(a, b):
    return jnp.dot(a, b, preferred_element_type=F32)


def _dot_nt(a, b):
    return lax.dot_general(a, b, (((1,), (1,)), ((), ())), preferred_element_type=F32)


def _split2(x):
    hi = x.astype(BF16)
    lo = (x - hi.astype(F32)).astype(BF16)
    return hi, lo


def _dot3(a, b):
    ah, al = _split2(a)
    bh, bl = _split2(b)
    return _dot(ah, bh) + (_dot(ah, bl) + _dot(al, bh))


def _rms(x, g):
    return x * lax.rsqrt(jnp.mean(x * x, axis=-1, keepdims=True) + EPS) * g


def _head_mask(shape):
    return lax.broadcasted_iota(jnp.int32, shape, len(shape) - 1) < HEAD_DIM


def _full_spec(shape):
    nd = len(shape)
    return pl.BlockSpec(shape, lambda *_: (0,) * nd)


def _rope_tables(S):
    half = ROT_DIM // 2
    inv_freq = ROPE_THETA ** (-jnp.arange(half, dtype=F32) / half)
    ang = jnp.arange(S, dtype=F32)[:, None] * inv_freq[None, :]
    d = jnp.arange(LANES) % HEAD_DIM
    ang_l = ang[:, d % half]
    lo = (d < half)[None, :]
    hi = ((d >= half) & (d < ROT_DIM))[None, :]
    cos = jnp.where(lo | hi, jnp.cos(ang_l), 1.0)
    sin_up = jnp.where(hi, jnp.sin(ang_l), 0.0)
    sin_dn = jnp.where(lo, -jnp.sin(ang_l), 0.0)
    return cos, sin_up, sin_dn


_Q_SCALE = HEAD_DIM ** -0.5 * math.log2(math.e)
_QKV_GROUPS = ((True, _Q_SCALE), (True, 1.0), (False, 1.0),
               (True, _Q_SCALE), (True, 1.0), (False, 1.0))


def _qkv_kernel(x_ref, g_ref, w_ref, cos_ref, su_ref, sd_ref, o_ref, vt_ref):
    hn = _rms(x_ref[0], g_ref[...]).astype(BF16)
    cos, su, sd = cos_ref[...], su_ref[...], sd_ref[...]
    half = ROT_DIM // 2
    tk = vt_ref.shape[-1]
    for j, (rot, scale) in enumerate(_QKV_GROUPS):
        y = _dot(hn, w_ref[:, j * 512:(j + 1) * 512])
        for c in range(4):
            t = y[:, c * LANES:(c + 1) * LANES]
            if rot:
                t = t * cos + pltpu.roll(t, half, 1) * su + pltpu.roll(t, LANES - half, 1) * sd
            if scale != 1.0:
                t = t * scale
            if j < len(_QKV_GROUPS) - 1:
                o_ref[0, 4 * j + c] = t.astype(BF16)
            else:
                tt = t.T.astype(BF16)
                for i in range(vt_ref.shape[2]):
                    vt_ref[0, c, i] = tt[:, i * tk:(i + 1) * tk]


def _qkv_proj(x, g, w_in):
    B, S, C = x.shape
    tm = ROW_TILE
    cos, su, sd = _rope_tables(S)
    nblk = QKV_COLS // LANES - B_HEADS
    tab = pl.BlockSpec((tm, LANES), lambda b, i: (i, 0))
    return pl.pallas_call(
        _qkv_kernel,
        grid=(B, S // tm),
        in_specs=[pl.BlockSpec((1, tm, C), lambda b, i: (b, i, 0)),
                  _full_spec((1, C)), _full_spec((C, QKV_COLS)), tab, tab, tab],
        out_specs=[pl.BlockSpec((1, nblk, tm, LANES), lambda b, i: (b, 0, i, 0)),
                   pl.BlockSpec((1, B_HEADS, tm // DIFF_TK, LANES, DIFF_TK), lambda b, i: (b, 0, i, 0, 0))],
        out_shape=[jax.ShapeDtypeStruct((B, nblk, S, LANES), BF16),
                   jax.ShapeDtypeStruct((B, B_HEADS, S // DIFF_TK, LANES, DIFF_TK), BF16)],
        compiler_params=_cparams("parallel", "parallel"),
    )(x, g.reshape(1, C), w_in.astype(BF16), cos, su, sd)


def _attn_a_kernel(q_ref, k_ref, v_ref, o_ref, qf, kf, vf, acc0_s, acc1_s, m0_s, m1_s, *, S, pad):
    CH = 512

    def load(i, _):
        sl = pl.ds(pl.multiple_of(i * CH, CH), CH)
        dst = pl.ds(pl.multiple_of(pad + i * CH, CH), CH)
        qf[sl, :] = q_ref[0, 0, sl, :].astype(F32)
        kf[dst, :] = k_ref[0, 0, sl, :].astype(F32)
        vf[dst, :] = v_ref[0, 0, sl, :].astype(F32)
        return 0

    lax.fori_loop(0, S // CH, load, 0)
    kf[pl.ds(0, pad), :] = jnp.zeros((pad, LANES), F32)
    vf[pl.ds(0, pad), :] = jnp.zeros((pad, LANES), F32)

    h0 = _head_mask((BLK, LANES))
    qi = lax.broadcasted_iota(jnp.int32, (BLK, 2 * BLK), 0)
    kc = lax.broadcasted_iota(jnp.int32, (BLK, 2 * BLK), 1)
    dist = BLK + qi - kc
    band = (dist >= 0) & (dist <= BLK)
    bias = jnp.where(band, 0.0, NEG)
    bias_first = jnp.where(band & (kc >= BLK), 0.0, NEG)

    h0k = _head_mask((2 * BLK, LANES))
    heads = ((h0, h0k, acc0_s, m0_s), (jnp.logical_not(h0), jnp.logical_not(h0k), acc1_s, m1_s))
    U = ATTN_A_UNITS_PER_STEP

    for p, (window, d) in enumerate(sorted(A_PATTERNS, key=lambda wd: -wd[1])):
        assert window // d == BLK
        nb = S // (d * BLK)
        assert (d * nb) % U == 0

        def group(gi, _, p=p, d=d):
            loaded = []
            for i in range(U):
                u = gi * U + i
                r = u % d
                n = u // d
                if d > 1:
                    rows_q = pl.ds(r + n * (BLK * d), BLK, stride=d)
                    rows_k = pl.ds(pad + r + (n - 1) * (BLK * d), 2 * BLK, stride=d)
                else:
                    rows_q = pl.ds(pl.multiple_of(n * BLK, BLK), BLK)
                    rows_k = pl.ds(pl.multiple_of(pad + (n - 1) * BLK, BLK), 2 * BLK)
                old = [(acc[rows_q, :], m[rows_q, :]) for _, _, acc, m in heads] if p > 0 else None
                loaded.append((n, rows_q, qf[rows_q, :], kf[rows_k, :], vf[rows_k, :], old))
            chains = [(ui, h) for ui in range(U) for h in range(2)]
            nch = len(chains)
            kbs = [k.astype(BF16) for _, _, _, k, _, _ in loaded]
            biases = [jnp.where(n == 0, bias_first, bias) for n, *_ in loaded]
            sc, sm, res = [None] * nch, [None] * nch, [None] * nch
            for i in range(nch + 2):
                if i < nch:
                    ui, h = chains[i]
                    qh = jnp.where(heads[h][0], loaded[ui][2], 0.0).astype(BF16)
                    sc[i] = _dot_nt(qh, kbs[ui]) + biases[ui]
                if 0 <= i - 1 < nch:
                    m_new = jnp.max(sc[i - 1], axis=-1, keepdims=True)
                    sm[i - 1] = (m_new, jnp.exp2(sc[i - 1] - m_new).astype(BF16))
                if 0 <= i - 2 < nch:
                    ui, h = chains[i - 2]
                    m_new, pb = sm[i - 2]
                    pv = _dot(pb, jnp.where(heads[h][1], loaded[ui][4], 1.0).astype(BF16))
                    if p == 0:
                        res[i - 2] = (pv, jnp.broadcast_to(m_new, (BLK, LANES)))
                    else:
                        a_old, m_old = loaded[ui][5][h]
                        m_tot = jnp.maximum(m_old, m_new)
                        res[i - 2] = (a_old * jnp.exp2(m_old - m_tot) + pv * jnp.exp2(m_new - m_tot), m_tot)
            for (ui, h), (a_val, m_val) in zip(chains, res):
                rows_q = loaded[ui][1]
                heads[h][2][rows_q, :] = a_val
                heads[h][3][rows_q, :] = m_val
            return 0

        lax.fori_loop(0, d * nb // U, group, 0)

    def store(i, _):
        sl = pl.ds(pl.multiple_of(i * BLK, BLK), BLK)
        a0 = acc0_s[sl, :]
        a1 = acc1_s[sl, :]
        o = jnp.where(h0, a0 / pltpu.roll(a0, HEAD_DIM, 1), a1 / pltpu.roll(a1, HEAD_DIM, 1))
        o_ref[0, 0, sl, :] = o.astype(BF16)
        return 0

    lax.fori_loop(0, S // BLK, store, 0)


def _attn_a(qkv):
    B, _, S, _ = qkv.shape
    npair = A_WIDTH // LANES
    pad = BLK * A_PATTERNS[-1][1]
    assert S % pad == 0 and S // pad >= 2
    blk = lambda off: pl.BlockSpec((1, 1, S, LANES), lambda b, h: (b, off + h, 0, 0))
    return pl.pallas_call(
        functools.partial(_attn_a_kernel, S=S, pad=pad),
        grid=(B, npair),
        in_specs=[blk(0), blk(npair), blk(2 * npair)],
        out_specs=blk(0),
        out_shape=jax.ShapeDtypeStruct((B, npair, S, LANES), BF16),
        scratch_shapes=[pltpu.VMEM((S, LANES), F32), pltpu.VMEM((S + pad, LANES), F32),
                        pltpu.VMEM((S + pad, LANES), F32)] + [pltpu.VMEM((S, LANES), F32)] * 4,
        compiler_params=_cparams("parallel", "parallel"),
    )(qkv, qkv, qkv)


def _attn_b_kernel(lam_ref, gain_ref, q_ref, k_ref, vt_ref, o_ref, m_s, l_s, acc_s, *, S, lam_init):
    TQ, TK = DIFF_TQ, DIFF_TK
    assert TQ == TK
    lp = lam_ref[...]
    lam = (jnp.exp(jnp.sum(lp[0:1] * lp[1:2], axis=-1, keepdims=True))
           - jnp.exp(jnp.sum(lp[2:3] * lp[3:4], axis=-1, keepdims=True)) + lam_init)
    nq = S // TQ
    h0 = _head_mask((TQ, LANES))
    causal = (lax.broadcasted_iota(jnp.int32, (TK, TQ), 1) >= lax.broadcasted_iota(jnp.int32, (TK, TQ), 0))
    zero = jnp.zeros((), BF16)

    m_s[...] = jnp.full(m_s.shape, NEG, F32)
    l_s[...] = jnp.zeros(l_s.shape, F32)
    acc_s[...] = jnp.zeros(acc_s.shape, F32)

    def load_tile(qi):
        q = q_ref[0, 0, pl.ds(pl.multiple_of(qi * TQ, TQ), TQ), :]
        return q, [(m_s[qi, mp], l_s[qi, mp], acc_s[qi, mp]) for mp in range(2)]

    def run_tiles(kb, vt, tiles):
        chains = [(jnp.where(h0, q, zero) if mp == 0 else jnp.where(h0, zero, q), old[mp], masked)
                  for q, old, masked in tiles for mp in range(2)]
        n = len(chains)
        st, sm, out = [None] * n, [None] * n, [None] * n
        for i in range(n + 2):
            if i < n:
                qm, _, masked = chains[i]
                s = _dot_nt(kb, qm)
                st[i] = jnp.where(causal, s, NEG) if masked else s
            if 0 <= i - 1 < n:
                m, l, _ = chains[i - 1][1]
                m_new = jnp.maximum(m, jnp.max(st[i - 1], axis=0, keepdims=True))
                alpha = jnp.exp2(m - m_new)
                pt = jnp.exp2(st[i - 1] - m_new)
                sm[i - 1] = (m_new, alpha, alpha * l + jnp.sum(pt, axis=0, keepdims=True), pt.astype(BF16))
            if 0 <= i - 2 < n:
                m_new, alpha, l_new, pt = sm[i - 2]
                out[i - 2] = (m_new, l_new, alpha * chains[i - 2][1][2] + _dot(vt, pt))
        return [out[2 * t:2 * t + 2] for t in range(len(tiles))]

    def store_tile(qi, new):
        for mp, (m, l, acc) in enumerate(new):
            m_s[qi, mp] = m
            l_s[qi, mp] = l
            acc_s[qi, mp] = acc

    def finish(qi, stats):
        (_, l0, acc0), (_, l1, acc1) = stats
        o = acc0 / l0 - lam * (acc1 / l1)
        o = o * lax.rsqrt(jnp.mean(o * o, axis=0, keepdims=True) + EPS) * (gain_ref[...] * (1.0 - lam_init))
        o_ref[0, 0, pl.ds(pl.multiple_of(qi * TQ, TQ), TQ), :] = o.T.astype(BF16)

    def kv_block(kj, _):
        kb = k_ref[0, 0, pl.ds(pl.multiple_of(kj * TK, TK), TK), :]
        vt = vt_ref[0, 0, kj]
        n_below = nq - 1 - kj

        def pair(t, _):
            qa = kj + 1 + 2 * t
            loaded = [load_tile(qa), load_tile(qa + 1)]
            new = run_tiles(kb, vt, [(q, old, False) for q, old in loaded])
            store_tile(qa, new[0])
            store_tile(qa + 1, new[1])
            return 0

        lax.fori_loop(0, n_below // 2, pair, 0)

        def diag_and_last():
            loaded = [load_tile(kj), load_tile(nq - 1)]
            fin, new = run_tiles(kb, vt, [(*loaded[0], True), (*loaded[1], False)])
            store_tile(nq - 1, new)
            finish(kj, fin)

        def diag_only():
            finish(kj, run_tiles(kb, vt, [(*load_tile(kj), True)])[0])

        lax.cond(n_below % 2 == 1, diag_and_last, diag_only)
        return 0

    lax.fori_loop(0, nq, kv_block, 0)


def _attn_b(qkv, vt, diff_lambda, subln, lam_init):
    B, _, S, _ = qkv.shape
    base = 3 * A_WIDTH // LANES
    nq = S // DIFF_TQ
    blk = lambda off: pl.BlockSpec((1, 1, S, LANES), lambda b, h: (b, off + h, 0, 0))
    return pl.pallas_call(
        functools.partial(_attn_b_kernel, S=S, lam_init=lam_init),
        grid=(B, B_HEADS),
        in_specs=[_full_spec((4, HEAD_DIM)), _full_spec((2 * HEAD_DIM, 1)),
                  blk(base), blk(base + B_HEADS),
                  pl.BlockSpec((1, 1, S // DIFF_TK, LANES, DIFF_TK), lambda b, h: (b, h, 0, 0, 0))],
        out_specs=blk(0),
        out_shape=jax.ShapeDtypeStruct((B, B_HEADS, S, LANES), BF16),
        scratch_shapes=[pltpu.VMEM((nq, 2, 1, DIFF_TQ), F32), pltpu.VMEM((nq, 2, 1, DIFF_TQ), F32),
                        pltpu.VMEM((nq, 2, LANES, DIFF_TQ), F32)],
        compiler_params=_cparams("parallel", "parallel"),
    )(diff_lambda, subln.reshape(2 * HEAD_DIM, 1), qkv, qkv, vt)


def _mix_mlp_kernel(*refs, n_act, final_norm):
    res_ref = refs[0]
    act_refs = refs[1:1 + n_act]
    wo_ref, g_ref, w1_ref, w2_ref = refs[1 + n_act:5 + n_act]
    gf_ref = refs[5 + n_act] if final_norm else None
    o_ref = refs[-1]
    act = jnp.concatenate([a[0, c] for a in act_refs for c in range(a.shape[1])], axis=-1)
    h = res_ref[0] + _dot(act, wo_ref[...])
    hn = _rms(h, g_ref[...]).astype(BF16)
    fc = D_MODEL
    acc = jnp.zeros_like(h)
    for f in range(D_FF // fc):
        a = _dot(hn, w1_ref[:, f * fc:(f + 1) * fc])
        a = jnp.square(jnp.maximum(a, 0.0)).astype(BF16)
        acc = acc + _dot(a, w2_ref[f * fc:(f + 1) * fc, :])
    h = h + acc
    if final_norm:
        h = _rms(h, gf_ref[...])
    o_ref[0] = h


def _mix_mlp(res, acts, w_o, g, w1, w2, g_final=None):
    B, S, C = res.shape
    tm = ROW_TILE
    final_norm = g_final is not None
    row = pl.BlockSpec((1, tm, C), lambda b, i: (b, i, 0))
    in_specs = [row]
    for a in acts:
        in_specs.append(pl.BlockSpec((1, a.shape[1], tm, LANES), lambda b, i: (b, 0, i, 0)))
    in_specs += [_full_spec(w_o.shape), _full_spec((1, C)), _full_spec(w1.shape), _full_spec(w2.shape)]
    args = [res, *acts, w_o.astype(BF16), g.reshape(1, C), w1.astype(BF16), w2.astype(BF16)]
    if final_norm:
        in_specs.append(_full_spec((1, C)))
        args.append(g_final.reshape(1, C))
    return pl.pallas_call(
        functools.partial(_mix_mlp_kernel, n_act=len(acts), final_norm=final_norm),
        grid=(B, S // tm),
        in_specs=in_specs,
        out_specs=row,
        out_shape=jax.ShapeDtypeStruct((B, S, C), F32),
        compiler_params=_cparams("parallel", "parallel"),
    )(*args)


def _rwkv_proj_kernel(x_ref, xp_ref, g_ref, mu_ref, wr_ref, wk_ref, wv_ref, w1_ref, w2_ref,
                      a1_ref, a2_ref, g1_ref, g2_ref, w0_ref, a0_ref, kk_ref, ka_ref,
                      r_out, lw_out, k_out, v_out, kk_out, a_out, g_out):
    i = pl.program_id(1)
    g = g_ref[...]
    hn = _rms(x_ref[0], g)
    tm = hn.shape[0]
    last = _rms(xp_ref[0], g)[7:8, :]
    last = jnp.where(i == 0, 0.0, last)
    row = lax.broadcasted_iota(jnp.int32, hn.shape, 0)
    prev = jnp.where(row == 0, last, pltpu.roll(hn, 1, 0))
    xx = prev - hn

    def mixed(j):
        return (hn + xx * mu_ref[j:j + 1, :]).astype(BF16)

    r = _dot(mixed(0), wr_ref[...])
    wl = jnp.tanh(_dot(mixed(1), w1_ref[...])).astype(BF16)
    u = w0_ref[...] + _dot(wl, w2_ref[...])
    w = -(jnp.maximum(-u, 0.0) + jnp.log(1.0 + jnp.exp(-jnp.abs(u)))) - 0.5
    lw = -jnp.exp(w)
    k = _dot(mixed(2), wk_ref[...])
    v = _dot(mixed(3), wv_ref[...])
    al = _dot(mixed(4), a1_ref[...]).astype(BF16)
    a = jax.nn.sigmoid(a0_ref[...] + _dot(al, a2_ref[...]))
    gl = jax.nn.sigmoid(_dot(mixed(5), g1_ref[...])).astype(BF16)
    gate = _dot(gl, g2_ref[...])
    kk = k * kk_ref[...]
    k = k * (1.0 + (a - 1.0) * ka_ref[...])

    ri = lax.broadcasted_iota(jnp.int32, (LANES, LANES), 0) // HEAD_DIM
    ci = lax.broadcasted_iota(jnp.int32, (LANES, LANES), 1) // HEAD_DIM
    seg = jnp.where(ri == ci, 1.0, 0.0).astype(BF16)
    for c in range(N_LANE_BLOCKS):
        sl = slice(c * LANES, (c + 1) * LANES)
        kc = kk[:, sl]
        hi, lo = _split2(kc * kc)
        ss = _dot(hi, seg) + _dot(lo, seg)
        kk_out[0, c] = (kc / jnp.maximum(jnp.sqrt(ss), 1e-12)).astype(BF16)
        r_out[0, c] = r[:, sl].astype(BF16)
        lw_out[0, c] = lw[:, sl]
        k_out[0, c] = k[:, sl].astype(BF16)
        v_out[0, c] = v[:, sl].astype(BF16)
        a_out[0, c] = a[:, sl].astype(BF16)
        g_out[0, c] = gate[:, sl].astype(BF16)


def _rwkv_proj(h, g, mu, w_r, w_k, w_v, w1, w2, a1, a2, g1, g2, w0, a0, k_k, k_a):
    B, S, C = h.shape
    tm = RWKV_ROW_TILE
    glr = g1.shape[1]
    g1p = jnp.pad(g1, ((0, 0), (0, GATE_LORA_PAD - glr))).astype(BF16)
    g2p = jnp.pad(g2, ((0, GATE_LORA_PAD - glr), (0, 0))).astype(BF16)
    vec = lambda t: t.reshape(1, C)
    args = [h, h, vec(g), mu, w_r.astype(BF16), w_k.astype(BF16), w_v.astype(BF16),
            w1.astype(BF16), w2.astype(BF16), a1.astype(BF16), a2.astype(BF16), g1p, g2p,
            vec(w0), vec(a0), vec(k_k), vec(k_a)]
    in_specs = [pl.BlockSpec((1, tm, C), lambda b, i: (b, i, 0)),
                pl.BlockSpec((1, 8, C), lambda b, i: (b, jnp.maximum(i * (tm // 8) - 1, 0), 0))]
    in_specs += [_full_spec(a.shape) for a in args[2:]]
    out_spec = pl.BlockSpec((1, N_LANE_BLOCKS, tm, LANES), lambda b, i: (b, 0, i, 0))
    shp = lambda dt: jax.ShapeDtypeStruct((B, N_LANE_BLOCKS, S, LANES), dt)
    return pl.pallas_call(
        _rwkv_proj_kernel,
        grid=(B, S // tm),
        in_specs=in_specs,
        out_specs=[out_spec] * 7,
        out_shape=[shp(BF16), shp(F32), shp(BF16), shp(BF16), shp(BF16), shp(BF16), shp(BF16)],
        compiler_params=_cparams("parallel", "parallel"),
    )(*args)


def _bmm(a, b):
    return lax.dot_general(a, b, (((2,), (1,)), ((0,), (0,))), preferred_element_type=F32)


def _bmm_nt(a, b):
    return lax.dot_general(a, b, (((2,), (2,)), ((0,), (0,))), preferred_element_type=F32)


def _bmm_tn(a, b):
    return lax.dot_general(a, b, (((1,), (1,)), ((0,), (0,))), preferred_element_type=F32)


def _bmm3(a, b):
    ah, al = _split2(a)
    bh, bl = _split2(b)
    return _bmm(ah, bh) + (_bmm(ah, bl) + _bmm(al, bh))


def _tri_inverse(n_mat, nilpotency):
    m = n_mat.shape[-1]
    eye = jnp.where(lax.broadcasted_iota(jnp.int32, (m, m), 0) == lax.broadcasted_iota(jnp.int32, (m, m), 1),
                    1.0, 0.0)
    steps = int(math.log2(nilpotency))
    nb = n_mat.astype(BF16)
    t = eye + n_mat
    p = _bmm(nb, nb)
    for k in range(1, steps):
        pb = p.astype(BF16)
        if k == steps - 1:
            t = t + _bmm(pb, t.astype(BF16))
        else:
            both = _bmm(pb, jnp.concatenate([pb, t.astype(BF16)], axis=2))
            p = both[:, :, :m]
            t = t + both[:, :, m:]
    return t


def _rwkv_kernel(r_ref, lw_ref, k_ref, v_ref, kk_ref, a_ref, g_ref, rk_ref, lnw_ref, lnb_ref,
                 o_ref, z_s):
    @pl.when(pl.program_id(1) == 0)
    def _():
        z_s[...] = jnp.zeros_like(z_s)

    _, NP, CB, L, _ = r_ref.shape
    NB = NP * CB

    def chunks(ref):
        return ref[0].astype(F32).reshape(NB, L, LANES)

    r, lw, k, v, kk, a = (chunks(t) for t in (r_ref, lw_ref, k_ref, v_ref, kk_ref, a_ref))

    m0 = _head_mask((1, L, LANES))

    def stack(t):
        return jnp.concatenate([jnp.where(m0, t, 0.0), jnp.where(m0, 0.0, t)], axis=1)

    def unstack(t):
        return t[:, :L] + t[:, L:]

    ti = lax.broadcasted_iota(jnp.int32, (L, L), 0)
    si = lax.broadcasted_iota(jnp.int32, (L, L), 1)
    tri = jnp.broadcast_to(jnp.where(ti >= si, 1.0, 0.0).astype(BF16), (NB, L, L))
    t2 = lax.broadcasted_iota(jnp.int32, (2 * L, 2 * L), 0) % L
    s2 = lax.broadcasted_iota(jnp.int32, (2 * L, 2 * L), 1) % L
    strict = t2 > s2
    incl = t2 >= s2
    ri = lax.broadcasted_iota(jnp.int32, (LANES, LANES), 0)
    ci = lax.broadcasted_iota(jnp.int32, (LANES, LANES), 1)
    diag = ri == ci
    seg = jnp.where((ri // HEAD_DIM) == (ci // HEAD_DIM), 1.0, 0.0).astype(BF16)

    hi, lo = _split2(lw)
    cs = _bmm(jnp.concatenate([tri, tri], axis=2), jnp.concatenate([hi, lo], axis=1))
    tot = cs[:, L - 1:L, :]
    p_tot = jnp.exp(tot)
    kka = kk * a
    a2 = stack(-(kk * jnp.exp(cs - lw)))
    r2 = stack(r * jnp.exp(cs))
    p_inv = jnp.exp(-cs)
    b2 = stack(kka * p_inv).astype(BF16)
    k2 = stack(k * p_inv).astype(BF16)
    p_rest = jnp.exp(tot - cs)
    bh2 = stack(kka * p_rest).astype(BF16)
    kh2 = stack(k * p_rest).astype(BF16)
    v2 = stack(v).astype(BF16)

    ar = jnp.concatenate([a2, r2], axis=1).astype(BF16)
    bk = jnp.concatenate([b2, k2], axis=1)
    m = _bmm_nt(ar, bk)
    H = 2 * L
    n_ab = jnp.where(strict, m[:, :H, :H], 0.0)
    a_ak = jnp.where(strict, m[:, :H, H:], 0.0).astype(BF16)
    a_rb = jnp.where(incl, m[:, H:, :H], 0.0).astype(BF16)
    a_rk = jnp.where(incl, m[:, H:, H:], 0.0).astype(BF16)
    t_inv = _tri_inverse(n_ab, L).astype(BF16)
    akv = _bmm(a_ak, v2)
    x = jnp.concatenate([a2, akv], axis=2).astype(BF16)
    wu = _bmm(t_inv, x).astype(BF16)
    wuv = jnp.concatenate([wu, jnp.concatenate([jnp.zeros_like(v2), v2], axis=2)], axis=1)
    rbw = _bmm(jnp.concatenate([a_rb, a_rk], axis=2), wuv)
    rw = unstack(r2 + rbw[:, :, :LANES]).astype(BF16)
    yv = unstack(rbw[:, :, LANES:])
    gw = _bmm_tn(jnp.concatenate([bh2, kh2], axis=1), wuv)
    g_mat = gw[:, :, :LANES] + jnp.where(diag, p_tot, 0.0)
    h_mat = gw[:, :, LANES:]

    rw = rw.reshape(NP, CB, L, LANES)
    yv = yv.reshape(NP, CB, L, LANES)
    g_hi, g_lo = _split2(g_mat)
    g_hl = jnp.concatenate([g_hi, g_lo], axis=2).reshape(NP, CB, LANES, 2 * LANES)
    h_mat = h_mat.reshape(NP, CB, LANES, LANES)
    z = z_s[...]
    ys = []
    for c in range(CB):
        z_hi, z_lo = _split2(z)
        ys.append(_bmm(rw[:, c], z_hi) + yv[:, c])
        zz = jnp.concatenate([jnp.concatenate([z_hi, z_lo], axis=2),
                              jnp.concatenate([z_hi, jnp.zeros_like(z_lo)], axis=2)], axis=1)
        gz = _bmm(g_hl[:, c], zz)
        z = (gz[:, :, :LANES] + gz[:, :, LANES:]) + h_mat[:, c]
    z_s[...] = z
    y = jnp.stack(ys, axis=1).reshape(NB * L, LANES)
    seg2 = jnp.concatenate([seg, seg], axis=0)

    def head_sum(t):
        return _dot(jnp.concatenate(_split2(t), axis=1), seg2)

    def rows(t):
        return t.reshape(NP, CB * L, LANES)

    inv_n = 1.0 / HEAD_DIM
    yc = y - head_sum(y) * inv_n
    var = head_sum(yc * yc) * inv_n
    yn = rows(yc * lax.rsqrt(var + GN_EPS)) * lnw_ref[...] + lnb_ref[...]
    rk = rows(r * k) * rk_ref[...]
    bonus = rows(_dot(rk.reshape(NB * L, LANES).astype(BF16), seg)) * rows(v)
    gate = g_ref[0].astype(F32).reshape(NP, CB * L, LANES)
    o_ref[0] = ((yn + bonus) * gate).astype(BF16).reshape(NP, CB, L, LANES)


def _rwkv_mix(r, lw, k, v, kk, a, gate, r_k, ln_w, ln_b):
    B, NP, S, _ = r.shape
    L, CB = RWKV_CHUNK, RWKV_CHUNKS_PER_STEP
    nc = S // L
    chunked = lambda t: t.reshape(B, NP, nc, L, LANES)
    blk = pl.BlockSpec((1, NP, CB, L, LANES), lambda b, c: (b, 0, c, 0, 0))
    par = _full_spec((NP, 1, LANES))
    pv = lambda t: t.reshape(NP, 1, LANES)
    out = pl.pallas_call(
        _rwkv_kernel,
        grid=(B, nc // CB),
        in_specs=[blk] * 7 + [par] * 3,
        out_specs=blk,
        out_shape=jax.ShapeDtypeStruct((B, NP, nc, L, LANES), BF16),
        scratch_shapes=[pltpu.VMEM((NP, LANES, LANES), F32)],
        compiler_params=_cparams("parallel", "arbitrary"),
    )(*(chunked(t) for t in (r, lw, k, v, kk, a, gate)), pv(r_k), pv(ln_w), pv(ln_b))
    return out.reshape(B, NP, S, LANES)


def kernel(x, norm_mix, norm_mlp, norm_final, attn_w_in, attn_w_out, diff_lambda, diff_subln, rwkv_mu, rwkv_w_r, rwkv_w_k, rwkv_w_v, rwkv_w_o, rwkv_w0, rwkv_w1, rwkv_w2, rwkv_a0, rwkv_a1, rwkv_a2, rwkv_g1, rwkv_g2, rwkv_k_k, rwkv_k_a, rwkv_r_k, rwkv_ln_w, rwkv_ln_b, mlp_w1, mlp_w2):
    lam_init = 0.8 - 0.6 * math.exp(-0.3 * 0)
    qkv, vt = _qkv_proj(x, norm_mix[0], attn_w_in[0])
    oa = _attn_a(qkv)
    ob = _attn_b(qkv, vt, diff_lambda[0], diff_subln[0], lam_init)
    h = _mix_mlp(x, [oa, ob], attn_w_out[0], norm_mlp[0], mlp_w1[0], mlp_w2[0])
    r, lw, k, v, kk, a, gate = _rwkv_proj(
        h, norm_mix[1], rwkv_mu[0], rwkv_w_r[0], rwkv_w_k[0], rwkv_w_v[0], rwkv_w1[0], rwkv_w2[0],
        rwkv_a1[0], rwkv_a2[0], rwkv_g1[0], rwkv_g2[0], rwkv_w0[0], rwkv_a0[0], rwkv_k_k[0], rwkv_k_a[0])
    mix = _rwkv_mix(r, lw, k, v, kk, a, gate, rwkv_r_k[0], rwkv_ln_w[0], rwkv_ln_b[0])
    return _mix_mlp(h, [mix], rwkv_w_o[0], norm_mlp[1], mlp_w1[1], mlp_w2[1], g_final=norm_final)
```

```python
import functools
import math

import jax
import jax.numpy as jnp
from jax import lax
from jax.experimental import pallas as pl
from jax.experimental.pallas import tpu as pltpu

F32 = jnp.float32
BF16 = jnp.bfloat16

D_MODEL = 1024
HEAD_DIM = 64
ROT_DIM = HEAD_DIM // 4
ROPE_THETA = 500000.0
BLK = 128
A_PATTERNS = ((128, 1), (512, 4), (2048, 16))
A_WIDTH = 512
B_HEADS = 4
B_WIDTH = 512
QKV_COLS = 3072
GN_EPS = 64e-5
D_FF = 4 * D_MODEL
EPS = 1e-5
GATE_LORA_PAD = 256

LANES = 128
N_LANE_BLOCKS = D_MODEL // LANES
NEG = -1e30
VMEM_LIMIT = 56 * 1024 * 1024

ROW_TILE = 512
RWKV_ROW_TILE = 256
RWKV_CHUNK = 64
RWKV_CHUNKS_PER_STEP = 4
DIFF_TQ = 512
DIFF_TK = 512
ATTN_A_UNITS_PER_STEP = 8


def _cparams(*semantics):
    return pltpu.CompilerParams(dimension_semantics=semantics, vmem_limit_bytes=VMEM_LIMIT)


def _dot(a, b):
    return jnp.dot(a, b, preferred_element_type=F32)


def _dot_nt(a, b):
    return lax.dot_general(a, b, (((1,), (1,)), ((), ())), preferred_element_type=F32)


def _split2(x):
    hi = x.astype(BF16)
    lo = (x - hi.astype(F32)).astype(BF16)
    return hi, lo


def _dot3(a, b):
    ah, al = _split2(a)
    bh, bl = _split2(b)
    return _dot(ah, bh) + (_dot(ah, bl) + _dot(al, bh))


def _rms(x, g):
    return x * lax.rsqrt(jnp.mean(x * x, axis=-1, keepdims=True) + EPS) * g


def _head_mask(shape):
    return lax.broadcasted_iota(jnp.int32, shape, len(shape) - 1) < HEAD_DIM


def _full_spec(shape):
    nd = len(shape)
    return pl.BlockSpec(shape, lambda *_: (0,) * nd)


def _rope_tables(S):
    half = ROT_DIM // 2
    inv_freq = ROPE_THETA ** (-jnp.arange(half, dtype=F32) / half)
    ang = jnp.arange(S, dtype=F32)[:, None] * inv_freq[None, :]
    d = jnp.arange(LANES) % HEAD_DIM
    ang_l = ang[:, d % half]
    lo = (d < half)[None, :]
    hi = ((d >= half) & (d < ROT_DIM))[None, :]
    cos = jnp.where(lo | hi, jnp.cos(ang_l), 1.0)
    sin_up = jnp.where(hi, jnp.sin(ang_l), 0.0)
    sin_dn = jnp.where(lo, -jnp.sin(ang_l), 0.0)
    return cos, sin_up, sin_dn


_Q_SCALE = HEAD_DIM ** -0.5 * math.log2(math.e)
_QKV_GROUPS = ((True, _Q_SCALE), (True, 1.0), (False, 1.0),
               (True, _Q_SCALE), (True, 1.0), (False, 1.0))


def _qkv_kernel(x_ref, g_ref, w_ref, cos_ref, su_ref, sd_ref, o_ref, vt_ref):
    hn = _rms(x_ref[0], g_ref[...]).astype(BF16)
    cos, su, sd = cos_ref[...], su_ref[...], sd_ref[...]
    half = ROT_DIM // 2
    tk = vt_ref.shape[-1]
    for j, (rot, scale) in enumerate(_QKV_GROUPS):
        y = _dot(hn, w_ref[:, j * 512:(j + 1) * 512])
        for c in range(4):
            t = y[:, c * LANES:(c + 1) * LANES]
            if rot:
                t = t * cos + pltpu.roll(t, half, 1) * su + pltpu.roll(t, LANES - half, 1) * sd
            if scale != 1.0:
                t = t * scale
            if j < len(_QKV_GROUPS) - 1:
                o_ref[0, 4 * j + c] = t.astype(BF16)
            else:
                tt = t.T.astype(BF16)
                for i in range(vt_ref.shape[2]):
                    vt_ref[0, c, i] = tt[:, i * tk:(i + 1) * tk]


def _qkv_proj(x, g, w_in):
    B, S, C = x.shape
    tm = ROW_TILE
    cos, su, sd = _rope_tables(S)
    nblk = QKV_COLS // LANES - B_HEADS
    tab = pl.BlockSpec((tm, LANES), lambda b, i: (i, 0))
    return pl.pallas_call(
        _qkv_kernel,
        grid=(B, S // tm),
        in_specs=[pl.BlockSpec((1, tm, C), lambda b, i: (b, i, 0)),
                  _full_spec((1, C)), _full_spec((C, QKV_COLS)), tab, tab, tab],
        out_specs=[pl.BlockSpec((1, nblk, tm, LANES), lambda b, i: (b, 0, i, 0)),
                   pl.BlockSpec((1, B_HEADS, tm // DIFF_TK, LANES, DIFF_TK), lambda b, i: (b, 0, i, 0, 0))],
        out_shape=[jax.ShapeDtypeStruct((B, nblk, S, LANES), BF16),
                   jax.ShapeDtypeStruct((B, B_HEADS, S // DIFF_TK, LANES, DIFF_TK), BF16)],
        compiler_params=_cparams("parallel", "parallel"),
    )(x, g.reshape(1, C), w_in.astype(BF16), cos, su, sd)


def _attn_a_kernel(q_ref, k_ref, v_ref, o_ref, qf, kf, vf, acc0_s, acc1_s, m0_s, m1_s, *, S, pad):
    CH = 512

    def load(i, _):
        sl = pl.ds(pl.multiple_of(i * CH, CH), CH)
        dst = pl.ds(pl.multiple_of(pad + i * CH, CH), CH)
        qf[sl, :] = q_ref[0, 0, sl, :].astype(F32)
        kf[dst, :] = k_ref[0, 0, sl, :].astype(F32)
        vf[dst, :] = v_ref[0, 0, sl, :].astype(F32)
        return 0

    lax.fori_loop(0, S // CH, load, 0)
    kf[pl.ds(0, pad), :] = jnp.zeros((pad, LANES), F32)
    vf[pl.ds(0, pad), :] = jnp.zeros((pad, LANES), F32)

    h0 = _head_mask((BLK, LANES))
    qi = lax.broadcasted_iota(jnp.int32, (BLK, 2 * BLK), 0)
    kc = lax.broadcasted_iota(jnp.int32, (BLK, 2 * BLK), 1)
    dist = BLK + qi - kc
    band = (dist >= 0) & (dist <= BLK)
    bias = jnp.where(band, 0.0, NEG)
    bias_first = jnp.where(band & (kc >= BLK), 0.0, NEG)

    h0k = _head_mask((2 * BLK, LANES))
    heads = ((h0, h0k, acc0_s, m0_s), (jnp.logical_not(h0), jnp.logical_not(h0k), acc1_s, m1_s))
    U = ATTN_A_UNITS_PER_STEP

    for p, (window, d) in enumerate(sorted(A_PATTERNS, key=lambda wd: -wd[1])):
        assert window // d == BLK
        nb = S // (d * BLK)
        assert (d * nb) % U == 0

        def group(gi, _, p=p, d=d):
            loaded = []
            for i in range(U):
                u = gi * U + i
                r = u % d
                n = u // d
                if d > 1:
                    rows_q = pl.ds(r + n * (BLK * d), BLK, stride=d)
                    rows_k = pl.ds(pad + r + (n - 1) * (BLK * d), 2 * BLK, stride=d)
                else:
                    rows_q = pl.ds(pl.multiple_of(n * BLK, BLK), BLK)
                    rows_k = pl.ds(pl.multiple_of(pad + (n - 1) * BLK, BLK), 2 * BLK)
                old = [(acc[rows_q, :], m[rows_q, :]) for _, _, acc, m in heads] if p > 0 else None
                loaded.append((n, rows_q, qf[rows_q, :], kf[rows_k, :], vf[rows_k, :], old))
            chains = [(ui, h) for ui in range(U) for h in range(2)]
            nch = len(chains)
            kbs = [k.astype(BF16) for _, _, _, k, _, _ in loaded]
            biases = [jnp.where(n == 0, bias_first, bias) for n, *_ in loaded]
            sc, sm, res = [None] * nch, [None] * nch, [None] * nch
            for i in range(nch + 2):
                if i < nch:
                    ui, h = chains[i]
                    qh = jnp.where(heads[h][0], loaded[ui][2], 0.0).astype(BF16)
                    sc[i] = _dot_nt(qh, kbs[ui]) + biases[ui]
                if 0 <= i - 1 < nch:
                    m_new = jnp.max(sc[i - 1], axis=-1, keepdims=True)
                    sm[i - 1] = (m_new, jnp.exp2(sc[i - 1] - m_new).astype(BF16))
                if 0 <= i - 2 < nch:
                    ui, h = chains[i - 2]
                    m_new, pb = sm[i - 2]
                    pv = _dot(pb, jnp.where(heads[h][1], loaded[ui][4], 1.0).astype(BF16))
                    if p == 0:
                        res[i - 2] = (pv, jnp.broadcast_to(m_new, (BLK, LANES)))
                    else:
                        a_old, m_old = loaded[ui][5][h]
                        m_tot = jnp.maximum(m_old, m_new)
                        res[i - 2] = (a_old * jnp.exp2(m_old - m_tot) + pv * jnp.exp2(m_new - m_tot), m_tot)
            for (ui, h), (a_val, m_val) in zip(chains, res):
                rows_q = loaded[ui][1]
                heads[h][2][rows_q, :] = a_val
                heads[h][3][rows_q, :] = m_val
            return 0

        lax.fori_loop(0, d * nb // U, group, 0)

    h0c = _head_mask((CH, LANES))

    def store(i, _):
        sl = pl.ds(pl.multiple_of(i * CH, CH), CH)
        a0 = acc0_s[sl, :]
        a1 = acc1_s[sl, :]
        num = jnp.where(h0c, a0, a1)
        den = jnp.where(h0c, pltpu.roll(a0, HEAD_DIM, 1), pltpu.roll(a1, HEAD_DIM, 1))
        o_ref[0, 0, sl, :] = (num / den).astype(BF16)
        return 0

    lax.fori_loop(0, S // CH, store, 0)


def _attn_a(qkv):
    B, _, S, _ = qkv.shape
    npair = A_WIDTH // LANES
    pad = BLK * A_PATTERNS[-1][1]
    assert S % pad == 0 and S // pad >= 2
    blk = lambda off: pl.BlockSpec((1, 1, S, LANES), lambda b, h: (b, off + h, 0, 0))
    return pl.pallas_call(
        functools.partial(_attn_a_kernel, S=S, pad=pad),
        grid=(B, npair),
        in_specs=[blk(0), blk(npair), blk(2 * npair)],
        out_specs=blk(0),
        out_shape=jax.ShapeDtypeStruct((B, npair, S, LANES), BF16),
        scratch_shapes=[pltpu.VMEM((S, LANES), F32), pltpu.VMEM((S + pad, LANES), F32),
                        pltpu.VMEM((S + pad, LANES), F32)] + [pltpu.VMEM((S, LANES), F32)] * 4,
        compiler_params=_cparams("parallel", "parallel"),
    )(qkv, qkv, qkv)


def _attn_b_kernel(lam_ref, gain_ref, q_ref, k_ref, vt_ref, o_ref, m_s, l_s, acc_s, *, S, lam_init):
    TQ, TK = DIFF_TQ, DIFF_TK
    assert TQ == TK
    lp = lam_ref[...]
    lam = (jnp.exp(jnp.sum(lp[0:1] * lp[1:2], axis=-1, keepdims=True))
           - jnp.exp(jnp.sum(lp[2:3] * lp[3:4], axis=-1, keepdims=True)) + lam_init)
    nq = S // TQ
    h0 = _head_mask((TQ, LANES))
    causal = (lax.broadcasted_iota(jnp.int32, (TK, TQ), 1) >= lax.broadcasted_iota(jnp.int32, (TK, TQ), 0))
    zero = jnp.zeros((), BF16)

    m_s[...] = jnp.full(m_s.shape, NEG, F32)
    l_s[...] = jnp.zeros(l_s.shape, F32)
    acc_s[...] = jnp.zeros(acc_s.shape, F32)

    def load_tile(kj, qi, masked):
        kb = k_ref[0, 0, pl.ds(pl.multiple_of(kj * TK, TK), TK), :]
        vt = vt_ref[0, 0, kj]
        q = q_ref[0, 0, pl.ds(pl.multiple_of(qi * TQ, TQ), TQ), :]
        return kb, vt, q, [(m_s[qi, mp], l_s[qi, mp], acc_s[qi, mp]) for mp in range(2)], masked

    def run_tiles(tiles):
        chains = [(kb, vt, jnp.where(h0, q, zero) if mp == 0 else jnp.where(h0, zero, q), old[mp], masked)
                  for kb, vt, q, old, masked in tiles for mp in range(2)]
        n = len(chains)
        st, sm, out = [None] * n, [None] * n, [None] * n
        for i in range(n + 2):
            if i < n:
                kb, _, qm, _, masked = chains[i]
                s = _dot_nt(kb, qm)
                st[i] = jnp.where(causal, s, NEG) if masked else s
            if 0 <= i - 1 < n:
                m, l, _ = chains[i - 1][3]
                m_new = jnp.maximum(m, jnp.max(st[i - 1], axis=0, keepdims=True))
                alpha = jnp.exp2(m - m_new)
                pt = jnp.exp2(st[i - 1] - m_new)
                sm[i - 1] = (m_new, alpha, alpha * l + jnp.sum(pt, axis=0, keepdims=True), pt.astype(BF16))
            if 0 <= i - 2 < n:
                m_new, alpha, l_new, pt = sm[i - 2]
                _, vt, _, (_, _, acc), _ = chains[i - 2]
                out[i - 2] = (m_new, l_new, alpha * acc + _dot(vt, pt))
        return [out[2 * t:2 * t + 2] for t in range(len(tiles))]

    def store_tile(qi, new):
        for mp, (m, l, acc) in enumerate(new):
            m_s[qi, mp] = m
            l_s[qi, mp] = l
            acc_s[qi, mp] = acc

    def finish(qi, stats):
        (_, l0, acc0), (_, l1, acc1) = stats
        o = acc0 / l0 - lam * (acc1 / l1)
        o = o * lax.rsqrt(jnp.mean(o * o, axis=0, keepdims=True) + EPS) * (gain_ref[...] * (1.0 - lam_init))
        o_ref[0, 0, pl.ds(pl.multiple_of(qi * TQ, TQ), TQ), :] = o.T.astype(BF16)

    assert nq % 2 == 0
    half, ring = nq // 2, nq - 1

    def round_robin(g, _):
        pairs = [(g, ring)] + [((g + i) % ring, (g + ring - i) % ring) for i in range(1, half)]
        tiles = [(jnp.minimum(a, b), jnp.maximum(a, b)) for a, b in pairs]
        new = run_tiles([load_tile(kj, qi, False) for kj, qi in tiles])
        for (_, qi), stats in zip(tiles, new):
            store_tile(qi, stats)
        return 0

    lax.fori_loop(0, ring, round_robin, 0)

    for g in range(nq // half):
        blocks = range(g * half, (g + 1) * half)
        for qi, stats in zip(blocks, run_tiles([load_tile(qi, qi, True) for qi in blocks])):
            finish(qi, stats)


def _attn_b(qkv, vt, diff_lambda, subln, lam_init):
    B, _, S, _ = qkv.shape
    base = 3 * A_WIDTH // LANES
    nq = S // DIFF_TQ
    blk = lambda off: pl.BlockSpec((1, 1, S, LANES), lambda b, h: (b, off + h, 0, 0))
    return pl.pallas_call(
        functools.partial(_attn_b_kernel, S=S, lam_init=lam_init),
        grid=(B, B_HEADS),
        in_specs=[_full_spec((4, HEAD_DIM)), _full_spec((2 * HEAD_DIM, 1)),
                  blk(base), blk(base + B_HEADS),
                  pl.BlockSpec((1, 1, S // DIFF_TK, LANES, DIFF_TK), lambda b, h: (b, h, 0, 0, 0))],
        out_specs=blk(0),
        out_shape=jax.ShapeDtypeStruct((B, B_HEADS, S, LANES), BF16),
        scratch_shapes=[pltpu.VMEM((nq, 2, 1, DIFF_TQ), F32), pltpu.VMEM((nq, 2, 1, DIFF_TQ), F32),
                        pltpu.VMEM((nq, 2, LANES, DIFF_TQ), F32)],
        compiler_params=_cparams("parallel", "parallel"),
    )(diff_lambda, subln.reshape(2 * HEAD_DIM, 1), qkv, qkv, vt)


def _mix_mlp_kernel(*refs, n_act, final_norm):
    res_ref = refs[0]
    act_refs = refs[1:1 + n_act]
    wo_ref, g_ref, w1_ref, w2_ref = refs[1 + n_act:5 + n_act]
    gf_ref = refs[5 + n_act] if final_norm else None
    o_ref = refs[-1]
    act = jnp.concatenate([a[0, c] for a in act_refs for c in range(a.shape[1])], axis=-1)
    h = res_ref[0] + _dot(act, wo_ref[...])
    hn = _rms(h, g_ref[...]).astype(BF16)
    fc = D_MODEL
    acc = jnp.zeros_like(h)
    for f in range(D_FF // fc):
        a = _dot(hn, w1_ref[:, f * fc:(f + 1) * fc])
        a = jnp.square(jnp.maximum(a, 0.0)).astype(BF16)
        acc = acc + _dot(a, w2_ref[f * fc:(f + 1) * fc, :])
    h = h + acc
    if final_norm:
        h = _rms(h, gf_ref[...])
    o_ref[0] = h


def _mix_mlp(res, acts, w_o, g, w1, w2, g_final=None):
    B, S, C = res.shape
    tm = ROW_TILE
    final_norm = g_final is not None
    row = pl.BlockSpec((1, tm, C), lambda b, i: (b, i, 0))
    in_specs = [row]
    for a in acts:
        in_specs.append(pl.BlockSpec((1, a.shape[1], tm, LANES), lambda b, i: (b, 0, i, 0)))
    in_specs += [_full_spec(w_o.shape), _full_spec((1, C)), _full_spec(w1.shape), _full_spec(w2.shape)]
    args = [res, *acts, w_o.astype(BF16), g.reshape(1, C), w1.astype(BF16), w2.astype(BF16)]
    if final_norm:
        in_specs.append(_full_spec((1, C)))
        args.append(g_final.reshape(1, C))
    return pl.pallas_call(
        functools.partial(_mix_mlp_kernel, n_act=len(acts), final_norm=final_norm),
        grid=(B, S // tm),
        in_specs=in_specs,
        out_specs=row,
        out_shape=jax.ShapeDtypeStruct((B, S, C), F32),
        compiler_params=_cparams("parallel", "parallel"),
    )(*args)


def _rwkv_proj_kernel(x_ref, xp_ref, g_ref, mu_ref, wr_ref, wk_ref, wv_ref, w1_ref, w2_ref,
                      a1_ref, a2_ref, g1_ref, g2_ref, w0_ref, a0_ref, kk_ref, ka_ref,
                      r_out, lw_out, k_out, v_out, kk_out, a_out, g_out):
    i = pl.program_id(1)
    g = g_ref[...]
    hn = _rms(x_ref[0], g)
    tm = hn.shape[0]
    last = _rms(xp_ref[0], g)[7:8, :]
    last = jnp.where(i == 0, 0.0, last)
    row = lax.broadcasted_iota(jnp.int32, hn.shape, 0)
    prev = jnp.where(row == 0, last, pltpu.roll(hn, 1, 0))
    xx = prev - hn

    def mixed(j):
        return (hn + xx * mu_ref[j:j + 1, :]).astype(BF16)

    def put(out, val, dtype=BF16):
        for c in range(N_LANE_BLOCKS):
            out[0, c] = val[:, c * LANES:(c + 1) * LANES].astype(dtype)

    al = _dot(mixed(4), a1_ref[...]).astype(BF16)
    a = jax.nn.sigmoid(a0_ref[...] + _dot(al, a2_ref[...]))
    put(a_out, a)
    k = _dot(mixed(2), wk_ref[...])
    v = _dot(mixed(3), wv_ref[...])
    kk = k * kk_ref[...]
    put(k_out, k * (1.0 + (a - 1.0) * ka_ref[...]))
    ri = lax.broadcasted_iota(jnp.int32, (2 * LANES, 2 * LANES), 0) // HEAD_DIM
    ci = lax.broadcasted_iota(jnp.int32, (2 * LANES, 2 * LANES), 1) // HEAD_DIM
    seg = jnp.where(ri == ci, 1.0, 0.0).astype(BF16)
    for c in range(N_LANE_BLOCKS // 2):
        kc = kk[:, 2 * c * LANES:2 * (c + 1) * LANES]
        ss = _dot((kc * kc).astype(BF16), seg)
        kn = (kc * jnp.minimum(lax.rsqrt(ss), 1e12)).astype(BF16)
        kk_out[0, 2 * c] = kn[:, :LANES]
        kk_out[0, 2 * c + 1] = kn[:, LANES:]
    put(v_out, v)
    wl = jnp.tanh(_dot(mixed(1), w1_ref[...])).astype(BF16)
    u = w0_ref[...] + _dot(wl, w2_ref[...])
    r = _dot(mixed(0), wr_ref[...])
    w = -(jnp.maximum(-u, 0.0) + jnp.log(1.0 + jnp.exp(-jnp.abs(u)))) - 0.5
    put(lw_out, -jnp.exp(w), F32)
    put(r_out, r)
    gl = jax.nn.sigmoid(_dot(mixed(5), g1_ref[...])).astype(BF16)
    put(g_out, _dot(gl, g2_ref[...]))


def _rwkv_proj(h, g, mu, w_r, w_k, w_v, w1, w2, a1, a2, g1, g2, w0, a0, k_k, k_a):
    B, S, C = h.shape
    tm = RWKV_ROW_TILE
    glr = g1.shape[1]
    g1p = jnp.pad(g1, ((0, 0), (0, GATE_LORA_PAD - glr))).astype(BF16)
    g2p = jnp.pad(g2, ((0, GATE_LORA_PAD - glr), (0, 0))).astype(BF16)
    vec = lambda t: t.reshape(1, C)
    args = [h, h, vec(g), mu, w_r.astype(BF16), w_k.astype(BF16), w_v.astype(BF16),
            w1.astype(BF16), w2.astype(BF16), a1.astype(BF16), a2.astype(BF16), g1p, g2p,
            vec(w0), vec(a0), vec(k_k), vec(k_a)]
    in_specs = [pl.BlockSpec((1, tm, C), lambda b, i: (b, i, 0)),
                pl.BlockSpec((1, 8, C), lambda b, i: (b, jnp.maximum(i * (tm // 8) - 1, 0), 0))]
    in_specs += [_full_spec(a.shape) for a in args[2:]]
    out_spec = pl.BlockSpec((1, N_LANE_BLOCKS, tm, LANES), lambda b, i: (b, 0, i, 0))
    shp = lambda dt: jax.ShapeDtypeStruct((B, N_LANE_BLOCKS, S, LANES), dt)
    return pl.pallas_call(
        _rwkv_proj_kernel,
        grid=(B, S // tm),
        in_specs=in_specs,
        out_specs=[out_spec] * 7,
        out_shape=[shp(BF16), shp(F32), shp(BF16), shp(BF16), shp(BF16), shp(BF16), shp(BF16)],
        compiler_params=_cparams("parallel", "parallel"),
    )(*args)


def _bmm(a, b):
    return lax.dot_general(a, b, (((2,), (1,)), ((0,), (0,))), preferred_element_type=F32)


def _bmm_nt(a, b):
    return lax.dot_general(a, b, (((2,), (2,)), ((0,), (0,))), preferred_element_type=F32)


def _bmm_tn(a, b):
    return lax.dot_general(a, b, (((1,), (1,)), ((0,), (0,))), preferred_element_type=F32)


def _bmm3(a, b):
    ah, al = _split2(a)
    bh, bl = _split2(b)
    return _bmm(ah, bh) + (_bmm(ah, bl) + _bmm(al, bh))


TRI_BLOCK = 16


def _tri_inverse(n_mat, size):
    m = n_mat.shape[-1]
    ri = lax.broadcasted_iota(jnp.int32, (m, m), 0)
    ci = lax.broadcasted_iota(jnp.int32, (m, m), 1)
    eye = jnp.where(ri == ci, 1.0, 0.0)
    same = (ri // TRI_BLOCK) == (ci // TRI_BLOCK)
    db = jnp.where(same, n_mat, 0.0).astype(BF16)
    t = eye + jnp.where(same, n_mat, 0.0)
    p = _bmm(db, db)
    steps = int(math.log2(TRI_BLOCK))
    for k in range(1, steps):
        pb = p.astype(BF16)
        if k == steps - 1:
            t = t + _bmm(pb, t.astype(BF16))
        else:
            both = _bmm(pb, jnp.concatenate([pb, t.astype(BF16)], axis=2))
            p = both[:, :, :m]
            t = t + both[:, :, m:]
    b = TRI_BLOCK
    while b < size:
        wider = (ri // (2 * b)) == (ci // (2 * b))
        off = jnp.where(wider & jnp.logical_not(same), n_mat, 0.0).astype(BF16)
        tb = t.astype(BF16)
        t = t + _bmm(tb, _bmm(off, tb).astype(BF16))
        same = wider
        b *= 2
    return t


def _rwkv_kernel(r_ref, lw_ref, k_ref, v_ref, kk_ref, a_ref, g_ref, rk_ref, lnw_ref, lnb_ref,
                 o_ref, z_s):
    @pl.when(pl.program_id(1) == 0)
    def _():
        z_s[...] = jnp.zeros_like(z_s)

    _, NP, CB, L, _ = r_ref.shape
    NB = NP * CB

    def chunks(ref):
        return ref[0].astype(F32).reshape(NB, L, LANES)

    r, lw, k, v, kk, a = (chunks(t) for t in (r_ref, lw_ref, k_ref, v_ref, kk_ref, a_ref))

    m0 = _head_mask((1, L, LANES))

    def stack(t):
        return jnp.concatenate([jnp.where(m0, t, 0.0), jnp.where(m0, 0.0, t)], axis=1)

    def unstack(t):
        return t[:, :L] + t[:, L:]

    ti = lax.broadcasted_iota(jnp.int32, (L, L), 0)
    si = lax.broadcasted_iota(jnp.int32, (L, L), 1)
    tri = jnp.broadcast_to(jnp.where(ti >= si, 1.0, 0.0).astype(BF16), (NB, L, L))
    t2 = lax.broadcasted_iota(jnp.int32, (2 * L, 2 * L), 0) % L
    s2 = lax.broadcasted_iota(jnp.int32, (2 * L, 2 * L), 1) % L
    strict = t2 > s2
    incl = t2 >= s2
    ri = lax.broadcasted_iota(jnp.int32, (LANES, LANES), 0)
    ci = lax.broadcasted_iota(jnp.int32, (LANES, LANES), 1)
    diag = ri == ci
    seg = jnp.where((ri // HEAD_DIM) == (ci // HEAD_DIM), 1.0, 0.0).astype(BF16)

    hi, lo = _split2(lw)
    cs = _bmm(jnp.concatenate([tri, tri], axis=2), jnp.concatenate([hi, lo], axis=1))
    tot = cs[:, L - 1:L, :]
    p_tot = jnp.exp(tot)
    kka = kk * a
    a2 = stack(-(kk * jnp.exp(cs - lw)))
    r2 = stack(r * jnp.exp(cs))
    p_inv = jnp.exp(-cs)
    b2 = stack(kka * p_inv).astype(BF16)
    k2 = stack(k * p_inv).astype(BF16)
    p_rest = jnp.exp(tot - cs)
    bh2 = stack(kka * p_rest).astype(BF16)
    kh2 = stack(k * p_rest).astype(BF16)
    v2 = stack(v).astype(BF16)

    ar = jnp.concatenate([a2, r2], axis=1).astype(BF16)
    bk = jnp.concatenate([b2, k2], axis=1)
    m = _bmm_nt(ar, bk)
    H = 2 * L
    n_ab = jnp.where(strict, m[:, :H, :H], 0.0)
    a_ak = jnp.where(strict, m[:, :H, H:], 0.0).astype(BF16)
    a_rb = jnp.where(incl, m[:, H:, :H], 0.0).astype(BF16)
    a_rk = jnp.where(incl, m[:, H:, H:], 0.0).astype(BF16)
    t_inv = _tri_inverse(n_ab, L).astype(BF16)
    akv = _bmm(a_ak, v2)
    x = jnp.concatenate([a2, akv], axis=2).astype(BF16)
    wu = _bmm(t_inv, x).astype(BF16)
    wuv = jnp.concatenate([wu, jnp.concatenate([jnp.zeros_like(v2), v2], axis=2)], axis=1)
    rbw = _bmm(jnp.concatenate([a_rb, a_rk], axis=2), wuv)
    rw = unstack(r2 + rbw[:, :, :LANES]).astype(BF16)
    yv = unstack(rbw[:, :, LANES:])
    gw = _bmm_tn(jnp.concatenate([bh2, kh2], axis=1), wuv)
    g_mat = gw[:, :, :LANES] + jnp.where(diag, p_tot, 0.0)
    h_mat = gw[:, :, LANES:]

    rw = rw.reshape(NP, CB, L, LANES)
    yv = yv.reshape(NP, CB, L, LANES)
    g_hi, g_lo = _split2(g_mat)
    g_hl = jnp.concatenate([g_hi, g_lo], axis=2).reshape(NP, CB, LANES, 2 * LANES)
    h_mat = h_mat.reshape(NP, CB, LANES, LANES)
    z = z_s[...]
    ys = []
    for c in range(CB):
        z_hi, z_lo = _split2(z)
        ys.append(_bmm(rw[:, c], z_hi) + yv[:, c])
        zz = jnp.concatenate([jnp.concatenate([z_hi, z_lo], axis=2),
                              jnp.concatenate([z_hi, jnp.zeros_like(z_lo)], axis=2)], axis=1)
        gz = _bmm(g_hl[:, c], zz)
        z = (gz[:, :, :LANES] + gz[:, :, LANES:]) + h_mat[:, c]
    z_s[...] = z
    y = jnp.stack(ys, axis=1).reshape(NB * L, LANES)
    seg2 = jnp.concatenate([seg, seg], axis=0)

    def head_sum(t):
        return _dot(jnp.concatenate(_split2(t), axis=1), seg2)

    def rows(t):
        return t.reshape(NP, CB * L, LANES)

    inv_n = 1.0 / HEAD_DIM
    yc = y - head_sum(y) * inv_n
    var = head_sum(yc * yc) * inv_n
    yn = rows(yc * lax.rsqrt(var + GN_EPS)) * lnw_ref[...] + lnb_ref[...]
    rk = rows(r * k) * rk_ref[...]
    bonus = rows(_dot(rk.reshape(NB * L, LANES).astype(BF16), seg)) * rows(v)
    gate = g_ref[0].astype(F32).reshape(NP, CB * L, LANES)
    o_ref[0] = ((yn + bonus) * gate).astype(BF16).reshape(NP, CB, L, LANES)


def _rwkv_mix(r, lw, k, v, kk, a, gate, r_k, ln_w, ln_b):
    B, NP, S, _ = r.shape
    L, CB = RWKV_CHUNK, RWKV_CHUNKS_PER_STEP
    nc = S // L
    chunked = lambda t: t.reshape(B, NP, nc, L, LANES)
    blk = pl.BlockSpec((1, NP, CB, L, LANES), lambda b, c: (b, 0, c, 0, 0))
    par = _full_spec((NP, 1, LANES))
    pv = lambda t: t.reshape(NP, 1, LANES)
    out = pl.pallas_call(
        _rwkv_kernel,
        grid=(B, nc // CB),
        in_specs=[blk] * 7 + [par] * 3,
        out_specs=blk,
        out_shape=jax.ShapeDtypeStruct((B, NP, nc, L, LANES), BF16),
        scratch_shapes=[pltpu.VMEM((NP, LANES, LANES), F32)],
        compiler_params=_cparams("parallel", "arbitrary"),
    )(*(chunked(t) for t in (r, lw, k, v, kk, a, gate)), pv(r_k), pv(ln_w), pv(ln_b))
    return out.reshape(B, NP, S, LANES)


def kernel(x, norm_mix, norm_mlp, norm_final, attn_w_in, attn_w_out, diff_lambda, diff_subln, rwkv_mu, rwkv_w_r, rwkv_w_k, rwkv_w_v, rwkv_w_o, rwkv_w0, rwkv_w1, rwkv_w2, rwkv_a0, rwkv_a1, rwkv_a2, rwkv_g1, rwkv_g2, rwkv_k_k, rwkv_k_a, rwkv_r_k, rwkv_ln_w, rwkv_ln_b, mlp_w1, mlp_w2):
    lam_init = 0.8 - 0.6 * math.exp(-0.3 * 0)
    qkv, vt = _qkv_proj(x, norm_mix[0], attn_w_in[0])
    oa = _attn_a(qkv)
    ob = _attn_b(qkv, vt, diff_lambda[0], diff_subln[0], lam_init)
    h = _mix_mlp(x, [oa, ob], attn_w_out[0], norm_mlp[0], mlp_w1[0], mlp_w2[0])
    r, lw, k, v, kk, a, gate = _rwkv_proj(
        h, norm_mix[1], rwkv_mu[0], rwkv_w_r[0], rwkv_w_k[0], rwkv_w_v[0], rwkv_w1[0], rwkv_w2[0],
        rwkv_a1[0], rwkv_a2[0], rwkv_g1[0], rwkv_g2[0], rwkv_w0[0], rwkv_a0[0], rwkv_k_k[0], rwkv_k_a[0])
    mix = _rwkv_mix(r, lw, k, v, kk, a, gate, rwkv_r_k[0], rwkv_ln_w[0], rwkv_ln_b[0])
    return _mix_mlp(h, [mix], rwkv_w_o[0], norm_mlp[1], mlp_w1[1], mlp_w2[1], g_final=norm_final)
```

```python
import functools
import math

import jax
import jax.numpy as jnp
from jax import lax
from jax.experimental import pallas as pl
from jax.experimental.pallas import tpu as pltpu

F32 = jnp.float32
BF16 = jnp.bfloat16

D_MODEL = 1024
HEAD_DIM = 64
ROT_DIM = HEAD_DIM // 4
ROPE_THETA = 500000.0
BLK = 128
A_PATTERNS = ((128, 1), (512, 4), (2048, 16))
A_WIDTH = 512
B_HEADS = 4
B_WIDTH = 512
QKV_COLS = 3072
GN_EPS = 64e-5
D_FF = 4 * D_MODEL
EPS = 1e-5
GATE_LORA_PAD = 256

LANES = 128
N_LANE_BLOCKS = D_MODEL // LANES
NEG = -1e30
VMEM_LIMIT = 56 * 1024 * 1024

ROW_TILE = 512
RWKV_ROW_TILE = 512
RWKV_CHUNK = 64
RWKV_CHUNKS_PER_STEP = 4
DIFF_TQ = 512
DIFF_TK = 512
ATTN_A_UNITS_PER_STEP = 8


def _cparams(*semantics):
    return pltpu.CompilerParams(dimension_semantics=semantics, vmem_limit_bytes=VMEM_LIMIT)


def _dot(a, b):
    return jnp.dot(a, b, preferred_element_type=F32)


def _dot_nt(a, b):
    return lax.dot_general(a, b, (((1,), (1,)), ((), ())), preferred_element_type=F32)


def _split2(x):
    hi = x.astype(BF16)
    lo = (x - hi.astype(F32)).astype(BF16)
    return hi, lo


def _rms(x, g):
    return x * lax.rsqrt(jnp.mean(x * x, axis=-1, keepdims=True) + EPS) * g


def _head_mask(shape):
    return lax.broadcasted_iota(jnp.int32, shape, len(shape) - 1) < HEAD_DIM


def _full_spec(shape):
    nd = len(shape)
    return pl.BlockSpec(shape, lambda *_: (0,) * nd)


def _rope_tables(S):
    half = ROT_DIM // 2
    inv_freq = ROPE_THETA ** (-jnp.arange(half, dtype=F32) / half)
    ang = jnp.arange(S, dtype=F32)[:, None] * inv_freq[None, :]
    d = jnp.arange(LANES) % HEAD_DIM
    ang_l = ang[:, d % half]
    lo = (d < half)[None, :]
    hi = ((d >= half) & (d < ROT_DIM))[None, :]
    cos = jnp.where(lo | hi, jnp.cos(ang_l), 1.0)
    sin_up = jnp.where(hi, jnp.sin(ang_l), 0.0)
    sin_dn = jnp.where(lo, -jnp.sin(ang_l), 0.0)
    return cos, sin_up, sin_dn


_Q_SCALE = HEAD_DIM ** -0.5 * math.log2(math.e)
_QKV_GROUPS = ((True, _Q_SCALE), (True, 1.0), (False, 1.0),
               (True, _Q_SCALE), (True, 1.0), (False, 1.0))


def _qkv_kernel(x_ref, g_ref, w_ref, cos_ref, su_ref, sd_ref, o_ref, vt_ref):
    hn = _rms(x_ref[0], g_ref[...]).astype(BF16)
    cos, su, sd = cos_ref[...], su_ref[...], sd_ref[...]
    half = ROT_DIM // 2
    tk = vt_ref.shape[-1]
    for j, (rot, scale) in enumerate(_QKV_GROUPS):
        y = _dot(hn, w_ref[:, j * 512:(j + 1) * 512])
        for c in range(4):
            t = y[:, c * LANES:(c + 1) * LANES]
            if rot:
                t = t * cos + pltpu.roll(t, half, 1) * su + pltpu.roll(t, LANES - half, 1) * sd
            if scale != 1.0:
                t = t * scale
            if j < len(_QKV_GROUPS) - 1:
                o_ref[0, 4 * j + c] = t.astype(BF16)
            else:
                tt = t.T.astype(BF16)
                for i in range(vt_ref.shape[2]):
                    vt_ref[0, c, i] = tt[:, i * tk:(i + 1) * tk]


def _qkv_proj(x, g, w_in):
    B, S, C = x.shape
    tm = ROW_TILE
    cos, su, sd = _rope_tables(S)
    nblk = QKV_COLS // LANES - B_HEADS
    tab = pl.BlockSpec((tm, LANES), lambda b, i: (i, 0))
    return pl.pallas_call(
        _qkv_kernel,
        grid=(B, S // tm),
        in_specs=[pl.BlockSpec((1, tm, C), lambda b, i: (b, i, 0)),
                  _full_spec((1, C)), _full_spec((C, QKV_COLS)), tab, tab, tab],
        out_specs=[pl.BlockSpec((1, nblk, tm, LANES), lambda b, i: (b, 0, i, 0)),
                   pl.BlockSpec((1, B_HEADS, tm // DIFF_TK, LANES, DIFF_TK), lambda b, i: (b, 0, i, 0, 0))],
        out_shape=[jax.ShapeDtypeStruct((B, nblk, S, LANES), BF16),
                   jax.ShapeDtypeStruct((B, B_HEADS, S // DIFF_TK, LANES, DIFF_TK), BF16)],
        compiler_params=_cparams("parallel", "parallel"),
    )(x, g.reshape(1, C), w_in.astype(BF16), cos, su, sd)


def _attn_a_kernel(q_ref, k_ref, v_ref, o_ref, qf, kf, vf, acc0_s, acc1_s, m0_s, m1_s, *, S, pad):
    CH = 512

    def load(i, _):
        sl = pl.ds(pl.multiple_of(i * CH, CH), CH)
        dst = pl.ds(pl.multiple_of(pad + i * CH, CH), CH)
        qf[sl, :] = q_ref[0, 0, sl, :].astype(F32)
        kf[dst, :] = k_ref[0, 0, sl, :].astype(F32)
        vf[dst, :] = v_ref[0, 0, sl, :].astype(F32)
        return 0

    lax.fori_loop(0, S // CH, load, 0)
    kf[pl.ds(0, pad), :] = jnp.zeros((pad, LANES), F32)
    vf[pl.ds(0, pad), :] = jnp.zeros((pad, LANES), F32)

    h0 = _head_mask((BLK, LANES))
    qi = lax.broadcasted_iota(jnp.int32, (BLK, 2 * BLK), 0)
    kc = lax.broadcasted_iota(jnp.int32, (BLK, 2 * BLK), 1)
    dist = BLK + qi - kc
    band = (dist >= 0) & (dist <= BLK)
    bias = jnp.where(band, 0.0, NEG)
    bias_first = jnp.where(band & (kc >= BLK), 0.0, NEG)

    h0k = _head_mask((2 * BLK, LANES))
    zero, one = jnp.zeros((), BF16), jnp.ones((), BF16)
    heads = ((h0, h0k, acc0_s, m0_s), (jnp.logical_not(h0), jnp.logical_not(h0k), acc1_s, m1_s))
    U = ATTN_A_UNITS_PER_STEP

    for p, (window, d) in enumerate(sorted(A_PATTERNS, key=lambda wd: -wd[1])):
        assert window // d == BLK
        nb = S // (d * BLK)
        assert (d * nb) % U == 0

        def group(gi, _=None, p=p, d=d):
            static = isinstance(gi, int)
            aligned = (lambda x: x) if static else (lambda x: pl.multiple_of(x, BLK))
            loaded = []
            for i in range(U):
                u = gi * U + i
                r = u % d
                n = u // d
                if d > 1:
                    rows_q = pl.ds(r + n * (BLK * d), BLK, stride=d)
                    rows_k = pl.ds(pad + r + (n - 1) * (BLK * d), 2 * BLK, stride=d)
                else:
                    rows_q = pl.ds(aligned(n * BLK), BLK)
                    rows_k = pl.ds(aligned(pad + (n - 1) * BLK), 2 * BLK)
                old = [(acc[rows_q, :], m[rows_q, :]) for _, _, acc, m in heads] if p > 0 else None
                bias_n = bias_first if static and n == 0 else bias
                loaded.append((bias_n, rows_q, qf[rows_q, :].astype(BF16), kf[rows_k, :].astype(BF16),
                               vf[rows_k, :].astype(BF16), old))
            chains = [(ui, h) for ui in range(U) for h in range(2)]
            nch = len(chains)
            sc, sm, res = [None] * nch, [None] * nch, [None] * nch
            for i in range(nch + 2):
                if i < nch:
                    ui, h = chains[i]
                    qh = jnp.where(heads[h][0], loaded[ui][2], zero)
                    sc[i] = _dot_nt(qh, loaded[ui][3]) + loaded[ui][0]
                if 0 <= i - 1 < nch:
                    m_new = jnp.max(sc[i - 1], axis=-1, keepdims=True)
                    sm[i - 1] = (m_new, jnp.exp2(sc[i - 1] - m_new).astype(BF16))
                if 0 <= i - 2 < nch:
                    ui, h = chains[i - 2]
                    m_new, pb = sm[i - 2]
                    pv = _dot(pb, jnp.where(heads[h][1], loaded[ui][4], one))
                    if p == 0:
                        res[i - 2] = (pv, jnp.broadcast_to(m_new, (BLK, LANES)))
                    else:
                        a_old, m_old = loaded[ui][5][h]
                        m_tot = jnp.maximum(m_old, m_new)
                        res[i - 2] = (a_old * jnp.exp2(m_old - m_tot) + pv * jnp.exp2(m_new - m_tot), m_tot)
            for (ui, h), (a_val, m_val) in zip(chains, res):
                rows_q = loaded[ui][1]
                heads[h][2][rows_q, :] = a_val
                heads[h][3][rows_q, :] = m_val
            return 0

        n_first = -(-d // U)
        for gi in range(n_first):
            group(gi)
        lax.fori_loop(n_first, d * nb // U, group, 0)

    h0c = _head_mask((CH, LANES))

    def store(i, _):
        sl = pl.ds(pl.multiple_of(i * CH, CH), CH)
        a0 = acc0_s[sl, :]
        a1 = acc1_s[sl, :]
        num = jnp.where(h0c, a0, a1)
        den = jnp.where(h0c, pltpu.roll(a0, HEAD_DIM, 1), pltpu.roll(a1, HEAD_DIM, 1))
        o_ref[0, 0, sl, :] = (num / den).astype(BF16)
        return 0

    lax.fori_loop(0, S // CH, store, 0)


def _attn_a(qkv):
    B, _, S, _ = qkv.shape
    npair = A_WIDTH // LANES
    pad = BLK * A_PATTERNS[-1][1]
    assert S % pad == 0 and S // pad >= 2
    blk = lambda off: pl.BlockSpec((1, 1, S, LANES), lambda b, h: (b, off + h, 0, 0))
    return pl.pallas_call(
        functools.partial(_attn_a_kernel, S=S, pad=pad),
        grid=(B, npair),
        in_specs=[blk(0), blk(npair), blk(2 * npair)],
        out_specs=blk(0),
        out_shape=jax.ShapeDtypeStruct((B, npair, S, LANES), BF16),
        scratch_shapes=[pltpu.VMEM((S, LANES), F32), pltpu.VMEM((S + pad, LANES), F32),
                        pltpu.VMEM((S + pad, LANES), F32)] + [pltpu.VMEM((S, LANES), F32)] * 4,
        compiler_params=_cparams("parallel", "parallel"),
    )(qkv, qkv, qkv)


def _attn_b_kernel(lam_ref, gain_ref, q_ref, k_ref, vt_ref, o_ref, m_s, l_s, acc_s, *, S, lam_init):
    TQ, TK = DIFF_TQ, DIFF_TK
    assert TQ == TK
    lp = lam_ref[...]
    lam = (jnp.exp(jnp.sum(lp[0:1] * lp[1:2], axis=-1, keepdims=True))
           - jnp.exp(jnp.sum(lp[2:3] * lp[3:4], axis=-1, keepdims=True)) + lam_init)
    nq = S // TQ
    h0 = _head_mask((TQ, LANES))
    causal = (lax.broadcasted_iota(jnp.int32, (TK, TQ), 1) >= lax.broadcasted_iota(jnp.int32, (TK, TQ), 0))
    zero = jnp.zeros((), BF16)

    m_s[...] = jnp.full(m_s.shape, NEG, F32)
    l_s[...] = jnp.zeros(l_s.shape, F32)
    acc_s[...] = jnp.zeros(acc_s.shape, F32)

    def load_tile(kj, qi, masked):
        kb = k_ref[0, 0, pl.ds(pl.multiple_of(kj * TK, TK), TK), :]
        vt = vt_ref[0, 0, kj]
        q = q_ref[0, 0, pl.ds(pl.multiple_of(qi * TQ, TQ), TQ), :]
        return kb, vt, q, [(m_s[qi, mp], l_s[qi, mp], acc_s[qi, mp]) for mp in range(2)], masked

    def run_tiles(tiles):
        chains = [(kb, vt, jnp.where(h0, q, zero) if mp == 0 else jnp.where(h0, zero, q), old[mp], masked)
                  for kb, vt, q, old, masked in tiles for mp in range(2)]
        n = len(chains)
        st, sm, out = [None] * n, [None] * n, [None] * n
        for i in range(n + 2):
            if i < n:
                kb, _, qm, _, masked = chains[i]
                s = _dot_nt(kb, qm)
                st[i] = jnp.where(causal, s, NEG) if masked else s
            if 0 <= i - 1 < n:
                m, l, _ = chains[i - 1][3]
                m_new = jnp.maximum(m, jnp.max(st[i - 1], axis=0, keepdims=True))
                alpha = jnp.exp2(m - m_new)
                pt = jnp.exp2(st[i - 1] - m_new)
                sm[i - 1] = (m_new, alpha, alpha * l + jnp.sum(pt, axis=0, keepdims=True), pt.astype(BF16))
            if 0 <= i - 2 < n:
                m_new, alpha, l_new, pt = sm[i - 2]
                _, vt, _, (_, _, acc), _ = chains[i - 2]
                out[i - 2] = (m_new, l_new, alpha * acc + _dot(vt, pt))
        return [out[2 * t:2 * t + 2] for t in range(len(tiles))]

    def store_tile(qi, new):
        for mp, (m, l, acc) in enumerate(new):
            m_s[qi, mp] = m
            l_s[qi, mp] = l
            acc_s[qi, mp] = acc

    def finish(qi, stats):
        (_, l0, acc0), (_, l1, acc1) = stats
        o = acc0 / l0 - lam * (acc1 / l1)
        o = o * lax.rsqrt(jnp.mean(o * o, axis=0, keepdims=True) + EPS) * (gain_ref[...] * (1.0 - lam_init))
        o_ref[0, 0, pl.ds(pl.multiple_of(qi * TQ, TQ), TQ), :] = o.T.astype(BF16)

    assert nq % 2 == 0
    half, ring = nq // 2, nq - 1

    def round_robin(g, _):
        pairs = [(g, ring)] + [((g + i) % ring, (g + ring - i) % ring) for i in range(1, half)]
        tiles = [(jnp.minimum(a, b), jnp.maximum(a, b)) for a, b in pairs]
        new = run_tiles([load_tile(kj, qi, False) for kj, qi in tiles])
        for (_, qi), stats in zip(tiles, new):
            store_tile(qi, stats)
        return 0

    lax.fori_loop(0, ring, round_robin, 0)

    for g in range(nq // half):
        blocks = range(g * half, (g + 1) * half)
        for qi, stats in zip(blocks, run_tiles([load_tile(qi, qi, True) for qi in blocks])):
            finish(qi, stats)


def _attn_b(qkv, vt, diff_lambda, subln, lam_init):
    B, _, S, _ = qkv.shape
    base = 3 * A_WIDTH // LANES
    nq = S // DIFF_TQ
    blk = lambda off: pl.BlockSpec((1, 1, S, LANES), lambda b, h: (b, off + h, 0, 0))
    return pl.pallas_call(
        functools.partial(_attn_b_kernel, S=S, lam_init=lam_init),
        grid=(B, B_HEADS),
        in_specs=[_full_spec((4, HEAD_DIM)), _full_spec((2 * HEAD_DIM, 1)),
                  blk(base), blk(base + B_HEADS),
                  pl.BlockSpec((1, 1, S // DIFF_TK, LANES, DIFF_TK), lambda b, h: (b, h, 0, 0, 0))],
        out_specs=blk(0),
        out_shape=jax.ShapeDtypeStruct((B, B_HEADS, S, LANES), BF16),
        scratch_shapes=[pltpu.VMEM((nq, 2, 1, DIFF_TQ), F32), pltpu.VMEM((nq, 2, 1, DIFF_TQ), F32),
                        pltpu.VMEM((nq, 2, LANES, DIFF_TQ), F32)],
        compiler_params=_cparams("parallel", "parallel"),
    )(diff_lambda, subln.reshape(2 * HEAD_DIM, 1), qkv, qkv, vt)


def _mix_mlp_kernel(*refs, n_act, final_norm):
    res_ref = refs[0]
    act_refs = refs[1:1 + n_act]
    wo_ref, g_ref, w1_ref, w2_ref = refs[1 + n_act:5 + n_act]
    gf_ref = refs[5 + n_act] if final_norm else None
    o_ref = refs[-1]
    act = jnp.concatenate([a[0, c] for a in act_refs for c in range(a.shape[1])], axis=-1)
    h = res_ref[0] + _dot(act, wo_ref[...])
    hn = _rms(h, g_ref[...]).astype(BF16)
    fc = D_MODEL
    acc = jnp.zeros_like(h)
    for f in range(D_FF // fc):
        a = _dot(hn, w1_ref[:, f * fc:(f + 1) * fc])
        a = jnp.square(jnp.maximum(a, 0.0)).astype(BF16)
        acc = acc + _dot(a, w2_ref[f * fc:(f + 1) * fc, :])
    h = h + acc
    if final_norm:
        h = _rms(h, gf_ref[...])
    o_ref[0] = h


def _mix_mlp(res, acts, w_o, g, w1, w2, g_final=None):
    B, S, C = res.shape
    tm = ROW_TILE
    final_norm = g_final is not None
    row = pl.BlockSpec((1, tm, C), lambda b, i: (b, i, 0))
    in_specs = [row]
    for a in acts:
        in_specs.append(pl.BlockSpec((1, a.shape[1], tm, LANES), lambda b, i: (b, 0, i, 0)))
    in_specs += [_full_spec(w_o.shape), _full_spec((1, C)), _full_spec(w1.shape), _full_spec(w2.shape)]
    args = [res, *acts, w_o.astype(BF16), g.reshape(1, C), w1.astype(BF16), w2.astype(BF16)]
    if final_norm:
        in_specs.append(_full_spec((1, C)))
        args.append(g_final.reshape(1, C))
    return pl.pallas_call(
        functools.partial(_mix_mlp_kernel, n_act=len(acts), final_norm=final_norm),
        grid=(B, S // tm),
        in_specs=in_specs,
        out_specs=row,
        out_shape=jax.ShapeDtypeStruct((B, S, C), F32),
        compiler_params=_cparams("parallel", "parallel"),
    )(*args)


def _rwkv_proj_kernel(x_ref, xp_ref, g_ref, mu_ref, wr_ref, wk_ref, wv_ref, w1_ref, w2_ref,
                      a1_ref, a2_ref, g1_ref, g2_ref, w0_ref, a0_ref, kk_ref, ka_ref,
                      r_out, lw_out, k_out, v_out, kk_out, a_out, g_out):
    i = pl.program_id(1)
    g = g_ref[...]
    hn = _rms(x_ref[0], g)
    tm = hn.shape[0]
    last = _rms(xp_ref[0], g)[7:8, :]
    last = jnp.where(i == 0, 0.0, last)
    row = lax.broadcasted_iota(jnp.int32, hn.shape, 0)
    prev = jnp.where(row == 0, last, pltpu.roll(hn, 1, 0))
    xx = prev - hn

    def mixed(j):
        return (hn + xx * mu_ref[j:j + 1, :]).astype(BF16)

    def put(out, val, dtype=BF16):
        for c in range(N_LANE_BLOCKS):
            out[0, c] = val[:, c * LANES:(c + 1) * LANES].astype(dtype)

    al = _dot(mixed(4), a1_ref[...]).astype(BF16)
    a = jax.nn.sigmoid(a0_ref[...] + _dot(al, a2_ref[...]))
    put(a_out, a)
    k = _dot(mixed(2), wk_ref[...])
    v = _dot(mixed(3), wv_ref[...])
    kk = k * kk_ref[...]
    put(k_out, k * (1.0 + (a - 1.0) * ka_ref[...]))
    ri = lax.broadcasted_iota(jnp.int32, (2 * LANES, 2 * LANES), 0) // HEAD_DIM
    ci = lax.broadcasted_iota(jnp.int32, (2 * LANES, 2 * LANES), 1) // HEAD_DIM
    seg = jnp.where(ri == ci, 1.0, 0.0).astype(BF16)
    for c in range(N_LANE_BLOCKS // 2):
        kc = kk[:, 2 * c * LANES:2 * (c + 1) * LANES]
        ss = _dot((kc * kc).astype(BF16), seg)
        kn = (kc * jnp.minimum(lax.rsqrt(ss), 1e12)).astype(BF16)
        kk_out[0, 2 * c] = kn[:, :LANES]
        kk_out[0, 2 * c + 1] = kn[:, LANES:]
    put(v_out, v)
    wl = jnp.tanh(_dot(mixed(1), w1_ref[...])).astype(BF16)
    u = w0_ref[...] + _dot(wl, w2_ref[...])
    r = _dot(mixed(0), wr_ref[...])
    w = -(jnp.maximum(-u, 0.0) + jnp.log(1.0 + jnp.exp(-jnp.abs(u)))) - 0.5
    put(lw_out, -jnp.exp(w), F32)
    put(r_out, r)
    gl = jax.nn.sigmoid(_dot(mixed(5), g1_ref[...])).astype(BF16)
    put(g_out, _dot(gl, g2_ref[...]))


def _rwkv_proj(h, g, mu, w_r, w_k, w_v, w1, w2, a1, a2, g1, g2, w0, a0, k_k, k_a):
    B, S, C = h.shape
    tm = RWKV_ROW_TILE
    glr = g1.shape[1]
    g1p = jnp.pad(g1, ((0, 0), (0, GATE_LORA_PAD - glr))).astype(BF16)
    g2p = jnp.pad(g2, ((0, GATE_LORA_PAD - glr), (0, 0))).astype(BF16)
    vec = lambda t: t.reshape(1, C)
    args = [h, h, vec(g), mu, w_r.astype(BF16), w_k.astype(BF16), w_v.astype(BF16),
            w1.astype(BF16), w2.astype(BF16), a1.astype(BF16), a2.astype(BF16), g1p, g2p,
            vec(w0), vec(a0), vec(k_k), vec(k_a)]
    in_specs = [pl.BlockSpec((1, tm, C), lambda b, i: (b, i, 0)),
                pl.BlockSpec((1, 8, C), lambda b, i: (b, jnp.maximum(i * (tm // 8) - 1, 0), 0))]
    in_specs += [_full_spec(a.shape) for a in args[2:]]
    out_spec = pl.BlockSpec((1, N_LANE_BLOCKS, tm, LANES), lambda b, i: (b, 0, i, 0))
    shp = lambda dt: jax.ShapeDtypeStruct((B, N_LANE_BLOCKS, S, LANES), dt)
    return pl.pallas_call(
        _rwkv_proj_kernel,
        grid=(B, S // tm),
        in_specs=in_specs,
        out_specs=[out_spec] * 7,
        out_shape=[shp(BF16), shp(F32), shp(BF16), shp(BF16), shp(BF16), shp(BF16), shp(BF16)],
        compiler_params=_cparams("parallel", "parallel"),
    )(*args)


def _bmm(a, b):
    return lax.dot_general(a, b, (((2,), (1,)), ((0,), (0,))), preferred_element_type=F32)


def _bmm_nt(a, b):
    return lax.dot_general(a, b, (((2,), (2,)), ((0,), (0,))), preferred_element_type=F32)


def _bmm_tn(a, b):
    return lax.dot_general(a, b, (((1,), (1,)), ((0,), (0,))), preferred_element_type=F32)


TRI_BLOCK = 16


def _tri_inverse(n_mat, size):
    m = n_mat.shape[-1]
    ri = lax.broadcasted_iota(jnp.int32, (m, m), 0)
    ci = lax.broadcasted_iota(jnp.int32, (m, m), 1)
    eye = jnp.where(ri == ci, 1.0, 0.0)
    same = (ri // TRI_BLOCK) == (ci // TRI_BLOCK)
    db = jnp.where(same, n_mat, 0.0).astype(BF16)
    t = eye + jnp.where(same, n_mat, 0.0)
    p = _bmm(db, db)
    yield
    steps = int(math.log2(TRI_BLOCK))
    for k in range(1, steps):
        pb = p.astype(BF16)
        if k == steps - 1:
            t = t + _bmm(pb, t.astype(BF16))
        else:
            both = _bmm(pb, jnp.concatenate([pb, t.astype(BF16)], axis=2))
            p = both[:, :, :m]
            t = t + both[:, :, m:]
        yield
    b = TRI_BLOCK
    while b < size:
        wider = (ri // (2 * b)) == (ci // (2 * b))
        off = jnp.where(wider & jnp.logical_not(same), n_mat, 0.0).astype(BF16)
        tb = t.astype(BF16)
        ct = _bmm(off, tb).astype(BF16)
        yield
        t = t + _bmm(tb, ct)
        yield
        same = wider
        b *= 2
    return t


def _rwkv_kernel(r_ref, lw_ref, k_ref, v_ref, kk_ref, a_ref, g_ref, rk_ref, lnw_ref, lnb_ref,
                 o_ref, z_s):
    @pl.when(pl.program_id(1) == 0)
    def _():
        z_s[...] = jnp.zeros_like(z_s)

    for _ in _rwkv_pairs(r_ref, lw_ref, k_ref, v_ref, kk_ref, a_ref, g_ref, rk_ref, lnw_ref, lnb_ref,
                         o_ref, z_s, 0, r_ref.shape[1]):
        pass


def _rwkv_pairs(r_ref, lw_ref, k_ref, v_ref, kk_ref, a_ref, g_ref, rk_ref, lnw_ref, lnb_ref,
                o_ref, z_s, p0, p1):
    _, _, CB, L, _ = r_ref.shape
    NP = p1 - p0
    NB = NP * CB

    def chunks(ref):
        return ref[0, p0:p1].astype(F32).reshape(NB, L, LANES)

    r, lw, k, v, kk, a = (chunks(t) for t in (r_ref, lw_ref, k_ref, v_ref, kk_ref, a_ref))

    m0 = _head_mask((1, L, LANES))

    def stack(t):
        return jnp.concatenate([jnp.where(m0, t, 0.0), jnp.where(m0, 0.0, t)], axis=1)

    def unstack(t):
        return t[:, :L] + t[:, L:]

    ti = lax.broadcasted_iota(jnp.int32, (L, L), 0)
    si = lax.broadcasted_iota(jnp.int32, (L, L), 1)
    tri = jnp.broadcast_to(jnp.where(ti >= si, 1.0, 0.0).astype(BF16), (NB, L, L))
    t2 = lax.broadcasted_iota(jnp.int32, (2 * L, 2 * L), 0) % L
    s2 = lax.broadcasted_iota(jnp.int32, (2 * L, 2 * L), 1) % L
    strict = t2 > s2
    incl = t2 >= s2
    ri = lax.broadcasted_iota(jnp.int32, (LANES, LANES), 0)
    ci = lax.broadcasted_iota(jnp.int32, (LANES, LANES), 1)
    diag = ri == ci

    hi, lo = _split2(lw)
    cs = _bmm(jnp.concatenate([tri, tri], axis=2), jnp.concatenate([hi, lo], axis=1))
    yield
    tot = cs[:, L - 1:L, :]
    p_tot = jnp.exp(tot)
    kka = kk * a
    a2 = stack(-(kk * jnp.exp(cs - lw)))
    r2 = stack(r * jnp.exp(cs))
    p_inv = jnp.exp(-cs)
    b2 = stack(kka * p_inv).astype(BF16)
    k2 = stack(k * p_inv).astype(BF16)
    yield
    p_rest = jnp.exp(tot - cs)
    bh2 = stack(kka * p_rest).astype(BF16)
    kh2 = stack(k * p_rest).astype(BF16)
    v2 = stack(v).astype(BF16)

    ar = jnp.concatenate([a2, r2], axis=1).astype(BF16)
    bk = jnp.concatenate([b2, k2], axis=1)
    m = _bmm_nt(ar, bk)
    yield
    H = 2 * L
    n_ab = jnp.where(strict, m[:, :H, :H], 0.0)
    a_ak = jnp.where(strict, m[:, :H, H:], 0.0).astype(BF16)
    a_rb = jnp.where(incl, m[:, H:, :H], 0.0).astype(BF16)
    a_rk = jnp.where(incl, m[:, H:, H:], 0.0).astype(BF16)
    akv = _bmm(a_ak, v2)
    yield
    t_inv = (yield from _tri_inverse(n_ab, L)).astype(BF16)
    x = jnp.concatenate([a2, akv], axis=2).astype(BF16)
    wu = _bmm(t_inv, x).astype(BF16)
    yield
    wuv = jnp.concatenate([wu, jnp.concatenate([jnp.zeros_like(v2), v2], axis=2)], axis=1)
    rbw = _bmm(jnp.concatenate([a_rb, a_rk], axis=2), wuv)
    yield
    rw = unstack(r2 + rbw[:, :, :LANES]).astype(BF16)
    yv = unstack(rbw[:, :, LANES:])
    gw = _bmm_tn(jnp.concatenate([bh2, kh2], axis=1), wuv)
    yield
    g_mat = gw[:, :, :LANES] + jnp.where(diag, p_tot, 0.0)
    h_mat = gw[:, :, LANES:]

    rw = rw.reshape(NP, CB, L, LANES)
    yv = yv.reshape(NP, CB, L, LANES)
    g_hi, g_lo = _split2(g_mat)
    g_hl = jnp.concatenate([g_hi, g_lo], axis=2).reshape(NP, CB, LANES, 2 * LANES)
    h_mat = h_mat.reshape(NP, CB, LANES, LANES)
    z = z_s[p0:p1]
    ys = []
    for c in range(CB):
        z_hi, z_lo = _split2(z)
        ys.append(_bmm(rw[:, c], z_hi) + yv[:, c])
        zz = jnp.concatenate([jnp.concatenate([z_hi, z_lo], axis=2),
                              jnp.concatenate([z_hi, jnp.zeros_like(z_lo)], axis=2)], axis=1)
        gz = _bmm(g_hl[:, c], zz)
        z = (gz[:, :, :LANES] + gz[:, :, LANES:]) + h_mat[:, c]
        yield
    z_s[p0:p1] = z
    y = jnp.stack(ys, axis=1).reshape(NB * L, LANES)
    hm = _head_mask((NB * L, LANES))

    def head_sum(t):
        s0 = jnp.sum(jnp.where(hm, t, 0.0), axis=-1, keepdims=True)
        s1 = jnp.sum(jnp.where(hm, 0.0, t), axis=-1, keepdims=True)
        return jnp.where(hm, s0, s1)

    def rows(t):
        return t.reshape(NP, CB * L, LANES)

    inv_n = 1.0 / HEAD_DIM
    yc = y - head_sum(y) * inv_n
    yield
    var = head_sum(yc * yc) * inv_n
    yn = rows(yc * lax.rsqrt(var + GN_EPS)) * lnw_ref[p0:p1] + lnb_ref[p0:p1]
    yield
    rk = rows(r * k) * rk_ref[p0:p1]
    bonus = rows(head_sum(rk.reshape(NB * L, LANES))) * rows(v)
    gate = g_ref[0, p0:p1].astype(F32).reshape(NP, CB * L, LANES)
    o_ref[0, p0:p1] = ((yn + bonus) * gate).astype(BF16).reshape(NP, CB, L, LANES)


def _rwkv_mix(r, lw, k, v, kk, a, gate, r_k, ln_w, ln_b):
    B, NP, S, _ = r.shape
    L, CB = RWKV_CHUNK, RWKV_CHUNKS_PER_STEP
    nc = S // L
    chunked = lambda t: t.reshape(B, NP, nc, L, LANES)
    blk = pl.BlockSpec((1, NP, CB, L, LANES), lambda b, c: (b, 0, c, 0, 0))
    par = _full_spec((NP, 1, LANES))
    pv = lambda t: t.reshape(NP, 1, LANES)
    out = pl.pallas_call(
        _rwkv_kernel,
        grid=(B, nc // CB),
        in_specs=[blk] * 7 + [par] * 3,
        out_specs=blk,
        out_shape=jax.ShapeDtypeStruct((B, NP, nc, L, LANES), BF16),
        scratch_shapes=[pltpu.VMEM((NP, LANES, LANES), F32)],
        compiler_params=_cparams("parallel", "arbitrary"),
    )(*(chunked(t) for t in (r, lw, k, v, kk, a, gate)), pv(r_k), pv(ln_w), pv(ln_b))
    return out.reshape(B, NP, S, LANES)


def kernel(x, norm_mix, norm_mlp, norm_final, attn_w_in, attn_w_out, diff_lambda, diff_subln, rwkv_mu, rwkv_w_r, rwkv_w_k, rwkv_w_v, rwkv_w_o, rwkv_w0, rwkv_w1, rwkv_w2, rwkv_a0, rwkv_a1, rwkv_a2, rwkv_g1, rwkv_g2, rwkv_k_k, rwkv_k_a, rwkv_r_k, rwkv_ln_w, rwkv_ln_b, mlp_w1, mlp_w2):
    lam_init = 0.8 - 0.6 * math.exp(-0.3 * 0)
    qkv, vt = _qkv_proj(x, norm_mix[0], attn_w_in[0])
    oa = _attn_a(qkv)
    ob = _attn_b(qkv, vt, diff_lambda[0], diff_subln[0], lam_init)
    h = _mix_mlp(x, [oa, ob], attn_w_out[0], norm_mlp[0], mlp_w1[0], mlp_w2[0])
    r, lw, k, v, kk, a, gate = _rwkv_proj(
        h, norm_mix[1], rwkv_mu[0], rwkv_w_r[0], rwkv_w_k[0], rwkv_w_v[0], rwkv_w1[0], rwkv_w2[0],
        rwkv_a1[0], rwkv_a2[0], rwkv_g1[0], rwkv_g2[0], rwkv_w0[0], rwkv_a0[0], rwkv_k_k[0], rwkv_k_a[0])
    mix = _rwkv_mix(r, lw, k, v, kk, a, gate, rwkv_r_k[0], rwkv_ln_w[0], rwkv_ln_b[0])
    return _mix_mlp(h, [mix], rwkv_w_o[0], norm_mlp[1], mlp_w1[1], mlp_w2[1], g_final=norm_final)
```

```python
import functools
import math

import jax
import jax.numpy as jnp
from jax import lax
from jax.experimental import pallas as pl
from jax.experimental.pallas import tpu as pltpu

F32 = jnp.float32
BF16 = jnp.bfloat16

D_MODEL = 1024
HEAD_DIM = 64
ROT_DIM = HEAD_DIM // 4
ROPE_THETA = 500000.0
BLK = 128
A_PATTERNS = ((128, 1), (512, 4), (2048, 16))
A_WIDTH = 512
B_HEADS = 4
B_WIDTH = 512
QKV_COLS = 3072
GN_EPS = 64e-5
D_FF = 4 * D_MODEL
EPS = 1e-5
GATE_LORA_PAD = 256

LANES = 128
N_LANE_BLOCKS = D_MODEL // LANES
NEG = -1e30
VMEM_LIMIT = 56 * 1024 * 1024

ROW_TILE = 512
RWKV_ROW_TILE = 512
RWKV_CHUNK = 64
RWKV_CHUNKS_PER_STEP = 4
DIFF_TQ = 512
DIFF_TK = 512
ATTN_A_UNITS_PER_STEP = 8


def _cparams(*semantics):
    return pltpu.CompilerParams(dimension_semantics=semantics, vmem_limit_bytes=VMEM_LIMIT)


def _dot(a, b):
    return jnp.dot(a, b, preferred_element_type=F32)


def _dot_nt(a, b):
    return lax.dot_general(a, b, (((1,), (1,)), ((), ())), preferred_element_type=F32)


def _split2(x):
    hi = x.astype(BF16)
    lo = (x - hi.astype(F32)).astype(BF16)
    return hi, lo


def _rms(x, g):
    return x * lax.rsqrt(jnp.mean(x * x, axis=-1, keepdims=True) + EPS) * g


def _head_mask(shape):
    return lax.broadcasted_iota(jnp.int32, shape, len(shape) - 1) < HEAD_DIM


def _full_spec(shape):
    nd = len(shape)
    return pl.BlockSpec(shape, lambda *_: (0,) * nd)


def _rope_tables(S):
    half = ROT_DIM // 2
    inv_freq = ROPE_THETA ** (-jnp.arange(half, dtype=F32) / half)
    ang = jnp.arange(S, dtype=F32)[:, None] * inv_freq[None, :]
    d = jnp.arange(LANES) % HEAD_DIM
    ang_l = ang[:, d % half]
    lo = (d < half)[None, :]
    hi = ((d >= half) & (d < ROT_DIM))[None, :]
    cos = jnp.where(lo | hi, jnp.cos(ang_l), 1.0)
    sin_up = jnp.where(hi, jnp.sin(ang_l), 0.0)
    sin_dn = jnp.where(lo, -jnp.sin(ang_l), 0.0)
    return cos, sin_up, sin_dn


_Q_SCALE = HEAD_DIM ** -0.5 * math.log2(math.e)
_QKV_GROUPS = ((True, _Q_SCALE), (True, 1.0), (False, 1.0),
               (True, _Q_SCALE), (True, 1.0), (False, 1.0))


def _qkv_kernel(x_ref, g_ref, w_ref, cos_ref, su_ref, sd_ref, o_ref, vt_ref):
    hn = _rms(x_ref[0], g_ref[...]).astype(BF16)
    cos, su, sd = cos_ref[...], su_ref[...], sd_ref[...]
    half = ROT_DIM // 2
    tk = vt_ref.shape[-1]
    gw = QKV_COLS // len(_QKV_GROUPS)
    for j, (rot, scale) in enumerate(_QKV_GROUPS):
        y = _dot(hn, w_ref[:, j * gw:(j + 1) * gw])
        for c in range(gw // LANES):
            t = y[:, c * LANES:(c + 1) * LANES]
            if rot:
                t = t * cos + pltpu.roll(t, half, 1) * su + pltpu.roll(t, LANES - half, 1) * sd
            if scale != 1.0:
                t = t * scale
            if j < len(_QKV_GROUPS) - 1:
                o_ref[0, (gw // LANES) * j + c] = t.astype(BF16)
            else:
                tt = t.T.astype(BF16)
                for i in range(vt_ref.shape[2]):
                    vt_ref[0, c, i] = tt[:, i * tk:(i + 1) * tk]


def _qkv_proj(x, g, w_in):
    B, S, C = x.shape
    tm = ROW_TILE
    cos, su, sd = _rope_tables(S)
    nblk = QKV_COLS // LANES - B_HEADS
    tab = pl.BlockSpec((tm, LANES), lambda b, i: (i, 0))
    return pl.pallas_call(
        _qkv_kernel,
        grid=(B, S // tm),
        in_specs=[pl.BlockSpec((1, tm, C), lambda b, i: (b, i, 0)),
                  _full_spec((1, C)), _full_spec((C, QKV_COLS)), tab, tab, tab],
        out_specs=[pl.BlockSpec((1, nblk, tm, LANES), lambda b, i: (b, 0, i, 0)),
                   pl.BlockSpec((1, B_HEADS, tm // DIFF_TK, LANES, DIFF_TK), lambda b, i: (b, 0, i, 0, 0))],
        out_shape=[jax.ShapeDtypeStruct((B, nblk, S, LANES), BF16),
                   jax.ShapeDtypeStruct((B, B_HEADS, S // DIFF_TK, LANES, DIFF_TK), BF16)],
        compiler_params=_cparams("parallel", "parallel"),
    )(x, g.reshape(1, C), w_in.astype(BF16), cos, su, sd)


def _attn_a_kernel(q_ref, k_ref, v_ref, o_ref, qf, kf, vf, acc0_s, acc1_s, m0_s, m1_s, *, S, pad):
    CH = 512

    def load(i, _):
        sl = pl.ds(pl.multiple_of(i * CH, CH), CH)
        dst = pl.ds(pl.multiple_of(pad + i * CH, CH), CH)
        qf[sl, :] = q_ref[0, 0, sl, :].astype(F32)
        kf[dst, :] = k_ref[0, 0, sl, :].astype(F32)
        vf[dst, :] = v_ref[0, 0, sl, :].astype(F32)
        return 0

    lax.fori_loop(0, S // CH, load, 0)
    kf[pl.ds(0, pad), :] = jnp.zeros((pad, LANES), F32)
    vf[pl.ds(0, pad), :] = jnp.zeros((pad, LANES), F32)

    h0 = _head_mask((BLK, LANES))
    qi = lax.broadcasted_iota(jnp.int32, (BLK, 2 * BLK), 0)
    kc = lax.broadcasted_iota(jnp.int32, (BLK, 2 * BLK), 1)
    dist = BLK + qi - kc
    band = (dist >= 0) & (dist <= BLK)
    bias = jnp.where(band, 0.0, NEG)
    bias_first = jnp.where(band & (kc >= BLK), 0.0, NEG)

    h0k = _head_mask((2 * BLK, LANES))
    zero, one = jnp.zeros((), BF16), jnp.ones((), BF16)
    heads = ((h0, h0k, acc0_s, m0_s), (jnp.logical_not(h0), jnp.logical_not(h0k), acc1_s, m1_s))
    U = ATTN_A_UNITS_PER_STEP

    for p, (window, d) in enumerate(sorted(A_PATTERNS, key=lambda wd: -wd[1])):
        assert window // d == BLK
        nb = S // (d * BLK)
        assert (d * nb) % U == 0

        def group(gi, _=None, p=p, d=d):
            static = isinstance(gi, int)
            aligned = (lambda x: x) if static else (lambda x: pl.multiple_of(x, BLK))
            loaded = []
            for i in range(U):
                u = gi * U + i
                r = u % d
                n = u // d
                if d > 1:
                    rows_q = pl.ds(r + n * (BLK * d), BLK, stride=d)
                    rows_k = pl.ds(pad + r + (n - 1) * (BLK * d), 2 * BLK, stride=d)
                else:
                    rows_q = pl.ds(aligned(n * BLK), BLK)
                    rows_k = pl.ds(aligned(pad + (n - 1) * BLK), 2 * BLK)
                old = [(acc[rows_q, :], m[rows_q, :]) for _, _, acc, m in heads] if p > 0 else None
                bias_n = bias_first if static and n == 0 else bias
                loaded.append((bias_n, rows_q, qf[rows_q, :].astype(BF16), kf[rows_k, :].astype(BF16),
                               vf[rows_k, :].astype(BF16), old))
            chains = [(ui, h) for ui in range(U) for h in range(2)]
            nch = len(chains)
            sc, sm, res = [None] * nch, [None] * nch, [None] * nch
            for i in range(nch + 2):
                if i < nch:
                    ui, h = chains[i]
                    qh = jnp.where(heads[h][0], loaded[ui][2], zero)
                    sc[i] = _dot_nt(qh, loaded[ui][3]) + loaded[ui][0]
                if 0 <= i - 1 < nch:
                    m_new = jnp.max(sc[i - 1], axis=-1, keepdims=True)
                    sm[i - 1] = (m_new, jnp.exp2(sc[i - 1] - m_new).astype(BF16))
                if 0 <= i - 2 < nch:
                    ui, h = chains[i - 2]
                    m_new, pb = sm[i - 2]
                    pv = _dot(pb, jnp.where(heads[h][1], loaded[ui][4], one))
                    if p == 0:
                        res[i - 2] = (pv, jnp.broadcast_to(m_new, (BLK, LANES)))
                    else:
                        a_old, m_old = loaded[ui][5][h]
                        m_tot = jnp.maximum(m_old, m_new)
                        res[i - 2] = (a_old * jnp.exp2(m_old - m_tot) + pv * jnp.exp2(m_new - m_tot), m_tot)
            for (ui, h), (a_val, m_val) in zip(chains, res):
                rows_q = loaded[ui][1]
                heads[h][2][rows_q, :] = a_val
                heads[h][3][rows_q, :] = m_val
            return 0

        n_first = -(-d // U)
        for gi in range(n_first):
            group(gi)
        lax.fori_loop(n_first, d * nb // U, group, 0)

    h0c = _head_mask((CH, LANES))

    def store(i, _):
        sl = pl.ds(pl.multiple_of(i * CH, CH), CH)
        a0 = acc0_s[sl, :]
        a1 = acc1_s[sl, :]
        num = jnp.where(h0c, a0, a1)
        den = jnp.where(h0c, pltpu.roll(a0, HEAD_DIM, 1), pltpu.roll(a1, HEAD_DIM, 1))
        o_ref[0, 0, sl, :] = (num / den).astype(BF16)
        return 0

    lax.fori_loop(0, S // CH, store, 0)


def _attn_a(qkv):
    B, _, S, _ = qkv.shape
    npair = A_WIDTH // LANES
    pad = BLK * A_PATTERNS[-1][1]
    assert S % pad == 0 and S // pad >= 2
    blk = lambda off: pl.BlockSpec((1, 1, S, LANES), lambda b, h: (b, off + h, 0, 0))
    return pl.pallas_call(
        functools.partial(_attn_a_kernel, S=S, pad=pad),
        grid=(B, npair),
        in_specs=[blk(0), blk(npair), blk(2 * npair)],
        out_specs=blk(0),
        out_shape=jax.ShapeDtypeStruct((B, npair, S, LANES), BF16),
        scratch_shapes=[pltpu.VMEM((S, LANES), F32), pltpu.VMEM((S + pad, LANES), F32),
                        pltpu.VMEM((S + pad, LANES), F32)] + [pltpu.VMEM((S, LANES), F32)] * 4,
        compiler_params=_cparams("parallel", "parallel"),
    )(qkv, qkv, qkv)


def _attn_b_kernel(lam_ref, gain_ref, q_ref, k_ref, vt_ref, o_ref, m_s, l_s, acc_s, *, S, lam_init):
    TQ, TK = DIFF_TQ, DIFF_TK
    assert TQ == TK
    lp = lam_ref[...]
    lam = (jnp.exp(jnp.sum(lp[0:1] * lp[1:2], axis=-1, keepdims=True))
           - jnp.exp(jnp.sum(lp[2:3] * lp[3:4], axis=-1, keepdims=True)) + lam_init)
    nq = S // TQ
    h0 = _head_mask((TQ, LANES))
    causal = (lax.broadcasted_iota(jnp.int32, (TK, TQ), 1) >= lax.broadcasted_iota(jnp.int32, (TK, TQ), 0))
    zero = jnp.zeros((), BF16)

    m_s[...] = jnp.full(m_s.shape, NEG, F32)
    l_s[...] = jnp.zeros(l_s.shape, F32)
    acc_s[...] = jnp.zeros(acc_s.shape, F32)

    def load_tile(kj, qi, masked):
        kb = k_ref[0, 0, pl.ds(pl.multiple_of(kj * TK, TK), TK), :]
        vt = vt_ref[0, 0, kj]
        q = q_ref[0, 0, pl.ds(pl.multiple_of(qi * TQ, TQ), TQ), :]
        return kb, vt, q, [(m_s[qi, mp], l_s[qi, mp], acc_s[qi, mp]) for mp in range(2)], masked

    def run_tiles(tiles):
        chains = [(kb, vt, jnp.where(h0, q, zero) if mp == 0 else jnp.where(h0, zero, q), old[mp], masked)
                  for kb, vt, q, old, masked in tiles for mp in range(2)]
        n = len(chains)
        st, sm, out = [None] * n, [None] * n, [None] * n
        for i in range(n + 2):
            if i < n:
                kb, _, qm, _, masked = chains[i]
                s = _dot_nt(kb, qm)
                st[i] = jnp.where(causal, s, NEG) if masked else s
            if 0 <= i - 1 < n:
                m, l, _ = chains[i - 1][3]
                m_new = jnp.maximum(m, jnp.max(st[i - 1], axis=0, keepdims=True))
                alpha = jnp.exp2(m - m_new)
                pt = jnp.exp2(st[i - 1] - m_new)
                sm[i - 1] = (m_new, alpha, alpha * l + jnp.sum(pt, axis=0, keepdims=True), pt.astype(BF16))
            if 0 <= i - 2 < n:
                m_new, alpha, l_new, pt = sm[i - 2]
                _, vt, _, (_, _, acc), _ = chains[i - 2]
                out[i - 2] = (m_new, l_new, alpha * acc + _dot(vt, pt))
        return [out[2 * t:2 * t + 2] for t in range(len(tiles))]

    def store_tile(qi, new):
        for mp, (m, l, acc) in enumerate(new):
            m_s[qi, mp] = m
            l_s[qi, mp] = l
            acc_s[qi, mp] = acc

    def finish(qi, stats):
        (_, l0, acc0), (_, l1, acc1) = stats
        o = acc0 / l0 - lam * (acc1 / l1)
        o = o * lax.rsqrt(jnp.mean(o * o, axis=0, keepdims=True) + EPS) * (gain_ref[...] * (1.0 - lam_init))
        o_ref[0, 0, pl.ds(pl.multiple_of(qi * TQ, TQ), TQ), :] = o.T.astype(BF16)

    assert nq % 2 == 0
    half, ring = nq // 2, nq - 1

    def round_robin(g, _):
        pairs = [(g, ring)] + [((g + i) % ring, (g + ring - i) % ring) for i in range(1, half)]
        tiles = [(jnp.minimum(a, b), jnp.maximum(a, b)) for a, b in pairs]
        new = run_tiles([load_tile(kj, qi, False) for kj, qi in tiles])
        for (_, qi), stats in zip(tiles, new):
            store_tile(qi, stats)
        return 0

    lax.fori_loop(0, ring, round_robin, 0)

    for g in range(nq // half):
        blocks = range(g * half, (g + 1) * half)
        for qi, stats in zip(blocks, run_tiles([load_tile(qi, qi, True) for qi in blocks])):
            finish(qi, stats)


def _attn_b(qkv, vt, diff_lambda, subln, lam_init):
    B, _, S, _ = qkv.shape
    base = 3 * A_WIDTH // LANES
    nq = S // DIFF_TQ
    blk = lambda off: pl.BlockSpec((1, 1, S, LANES), lambda b, h: (b, off + h, 0, 0))
    return pl.pallas_call(
        functools.partial(_attn_b_kernel, S=S, lam_init=lam_init),
        grid=(B, B_HEADS),
        in_specs=[_full_spec((4, HEAD_DIM)), _full_spec((2 * HEAD_DIM, 1)),
                  blk(base), blk(base + B_HEADS),
                  pl.BlockSpec((1, 1, S // DIFF_TK, LANES, DIFF_TK), lambda b, h: (b, h, 0, 0, 0))],
        out_specs=blk(0),
        out_shape=jax.ShapeDtypeStruct((B, B_HEADS, S, LANES), BF16),
        scratch_shapes=[pltpu.VMEM((nq, 2, 1, DIFF_TQ), F32), pltpu.VMEM((nq, 2, 1, DIFF_TQ), F32),
                        pltpu.VMEM((nq, 2, LANES, DIFF_TQ), F32)],
        compiler_params=_cparams("parallel", "parallel"),
    )(diff_lambda, subln.reshape(2 * HEAD_DIM, 1), qkv, qkv, vt)


def _mix_mlp_kernel(*refs, n_act, final_norm):
    res_ref = refs[0]
    act_refs = refs[1:1 + n_act]
    wo_ref, g_ref, w1_ref, w2_ref = refs[1 + n_act:5 + n_act]
    gf_ref = refs[5 + n_act] if final_norm else None
    o_ref = refs[-1]
    act = jnp.concatenate([a[0, c] for a in act_refs for c in range(a.shape[1])], axis=-1)
    h = res_ref[0] + _dot(act, wo_ref[...])
    hn = _rms(h, g_ref[...]).astype(BF16)
    fc = D_MODEL
    acc = jnp.zeros_like(h)
    for f in range(D_FF // fc):
        a = _dot(hn, w1_ref[:, f * fc:(f + 1) * fc])
        a = jnp.square(jnp.maximum(a, 0.0)).astype(BF16)
        acc = acc + _dot(a, w2_ref[f * fc:(f + 1) * fc, :])
    h = h + acc
    if final_norm:
        h = _rms(h, gf_ref[...])
    o_ref[0] = h


def _mix_mlp(res, acts, w_o, g, w1, w2, g_final=None):
    B, S, C = res.shape
    tm = ROW_TILE
    final_norm = g_final is not None
    row = pl.BlockSpec((1, tm, C), lambda b, i: (b, i, 0))
    in_specs = [row]
    for a in acts:
        in_specs.append(pl.BlockSpec((1, a.shape[1], tm, LANES), lambda b, i: (b, 0, i, 0)))
    in_specs += [_full_spec(w_o.shape), _full_spec((1, C)), _full_spec(w1.shape), _full_spec(w2.shape)]
    args = [res, *acts, w_o.astype(BF16), g.reshape(1, C), w1.astype(BF16), w2.astype(BF16)]
    if final_norm:
        in_specs.append(_full_spec((1, C)))
        args.append(g_final.reshape(1, C))
    return pl.pallas_call(
        functools.partial(_mix_mlp_kernel, n_act=len(acts), final_norm=final_norm),
        grid=(B, S // tm),
        in_specs=in_specs,
        out_specs=row,
        out_shape=jax.ShapeDtypeStruct((B, S, C), F32),
        compiler_params=_cparams("parallel", "parallel"),
    )(*args)


def _rwkv_proj_kernel(x_ref, xp_ref, g_ref, mu_ref, wr_ref, wk_ref, wv_ref, w1_ref, w2_ref,
                      a1_ref, a2_ref, g1_ref, g2_ref, w0_ref, a0_ref, kk_ref, ka_ref,
                      r_out, lw_out, k_out, v_out, kk_out, a_out, g_out):
    i = pl.program_id(1)
    g = g_ref[...]
    hn = _rms(x_ref[0], g)
    tm = hn.shape[0]
    last = _rms(xp_ref[0], g)[7:8, :]
    last = jnp.where(i == 0, 0.0, last)
    row = lax.broadcasted_iota(jnp.int32, hn.shape, 0)
    prev = jnp.where(row == 0, last, pltpu.roll(hn, 1, 0))
    xx = prev - hn

    def mixed(j):
        return (hn + xx * mu_ref[j:j + 1, :]).astype(BF16)

    def put(out, val, dtype=BF16):
        for c in range(N_LANE_BLOCKS):
            out[0, c] = val[:, c * LANES:(c + 1) * LANES].astype(dtype)

    al = _dot(mixed(4), a1_ref[...]).astype(BF16)
    a = jax.nn.sigmoid(a0_ref[...] + _dot(al, a2_ref[...]))
    put(a_out, a)
    k = _dot(mixed(2), wk_ref[...])
    v = _dot(mixed(3), wv_ref[...])
    kk = k * kk_ref[...]
    put(k_out, k * (1.0 + (a - 1.0) * ka_ref[...]))
    ri = lax.broadcasted_iota(jnp.int32, (2 * LANES, 2 * LANES), 0) // HEAD_DIM
    ci = lax.broadcasted_iota(jnp.int32, (2 * LANES, 2 * LANES), 1) // HEAD_DIM
    seg = jnp.where(ri == ci, 1.0, 0.0).astype(BF16)
    for c in range(N_LANE_BLOCKS // 2):
        kc = kk[:, 2 * c * LANES:2 * (c + 1) * LANES]
        ss = _dot((kc * kc).astype(BF16), seg)
        kn = (kc * jnp.minimum(lax.rsqrt(ss), 1e12)).astype(BF16)
        kk_out[0, 2 * c] = kn[:, :LANES]
        kk_out[0, 2 * c + 1] = kn[:, LANES:]
    put(v_out, v)
    wl = jnp.tanh(_dot(mixed(1), w1_ref[...])).astype(BF16)
    u = w0_ref[...] + _dot(wl, w2_ref[...])
    r = _dot(mixed(0), wr_ref[...])
    log2e = math.log2(math.e)
    w2 = jnp.minimum(u, 0.0) * log2e - jnp.log2(1.0 + jnp.exp2(jnp.abs(u) * -log2e)) - 0.5 * log2e
    put(lw_out, -jnp.exp2(w2), F32)
    put(r_out, r)
    gl = jax.nn.sigmoid(_dot(mixed(5), g1_ref[...])).astype(BF16)
    put(g_out, _dot(gl, g2_ref[...]))


def _rwkv_proj(h, g, mu, w_r, w_k, w_v, w1, w2, a1, a2, g1, g2, w0, a0, k_k, k_a):
    B, S, C = h.shape
    tm = RWKV_ROW_TILE
    glr = g1.shape[1]
    g1p = jnp.pad(g1, ((0, 0), (0, GATE_LORA_PAD - glr))).astype(BF16)
    g2p = jnp.pad(g2, ((0, GATE_LORA_PAD - glr), (0, 0))).astype(BF16)
    vec = lambda t: t.reshape(1, C)
    args = [h, h, vec(g), mu, w_r.astype(BF16), w_k.astype(BF16), w_v.astype(BF16),
            w1.astype(BF16), w2.astype(BF16), a1.astype(BF16), a2.astype(BF16), g1p, g2p,
            vec(w0), vec(a0), vec(k_k), vec(k_a)]
    in_specs = [pl.BlockSpec((1, tm, C), lambda b, i: (b, i, 0)),
                pl.BlockSpec((1, 8, C), lambda b, i: (b, jnp.maximum(i * (tm // 8) - 1, 0), 0))]
    in_specs += [_full_spec(a.shape) for a in args[2:]]
    out_spec = pl.BlockSpec((1, N_LANE_BLOCKS, tm, LANES), lambda b, i: (b, 0, i, 0))
    shp = lambda dt: jax.ShapeDtypeStruct((B, N_LANE_BLOCKS, S, LANES), dt)
    return pl.pallas_call(
        _rwkv_proj_kernel,
        grid=(B, S // tm),
        in_specs=in_specs,
        out_specs=[out_spec] * 7,
        out_shape=[shp(BF16), shp(F32), shp(BF16), shp(BF16), shp(BF16), shp(BF16), shp(BF16)],
        compiler_params=_cparams("parallel", "parallel"),
    )(*args)


def _bmm(a, b):
    return lax.dot_general(a, b, (((2,), (1,)), ((0,), (0,))), preferred_element_type=F32)


def _bmm_nt(a, b):
    return lax.dot_general(a, b, (((2,), (2,)), ((0,), (0,))), preferred_element_type=F32)


def _bmm_tn(a, b):
    return lax.dot_general(a, b, (((1,), (1,)), ((0,), (0,))), preferred_element_type=F32)


TRI_BLOCK = 16


def _tri_inverse(n_mat, size):
    m = n_mat.shape[-1]
    ri = lax.broadcasted_iota(jnp.int32, (m, m), 0)
    ci = lax.broadcasted_iota(jnp.int32, (m, m), 1)
    eye = jnp.where(ri == ci, 1.0, 0.0)
    same = (ri // TRI_BLOCK) == (ci // TRI_BLOCK)
    db = jnp.where(same, n_mat, 0.0).astype(BF16)
    t = eye + jnp.where(same, n_mat, 0.0)
    p = _bmm(db, db)
    yield
    steps = int(math.log2(TRI_BLOCK))
    for k in range(1, steps):
        pb = p.astype(BF16)
        if k == steps - 1:
            t = t + _bmm(pb, t.astype(BF16))
        else:
            both = _bmm(pb, jnp.concatenate([pb, t.astype(BF16)], axis=2))
            p = both[:, :, :m]
            t = t + both[:, :, m:]
        yield
    b = TRI_BLOCK
    while b < size:
        sel = [slice(s, s + b) for s in range(b, m, 2 * b)]

        def take(a, sel=sel):
            return jnp.concatenate([a[:, s] for s in sel], axis=1)

        wider = (ri // (2 * b)) == (ci // (2 * b))
        off = jnp.where(wider & jnp.logical_not(same), n_mat, 0.0).astype(BF16)
        tb = t.astype(BF16)
        ct = _bmm(take(off), tb).astype(BF16)
        yield
        gap = jnp.zeros((ct.shape[0], b, m), BF16)
        ct_rows = [piece for i in range(len(sel)) for piece in (gap, ct[:, i * b:(i + 1) * b])]
        upd = _bmm(take(tb), jnp.concatenate(ct_rows, axis=1))
        yield
        t_rows = [piece for i, s in enumerate(sel)
                  for piece in (t[:, s.start - b:s.start], t[:, s] + upd[:, i * b:(i + 1) * b])]
        t = jnp.concatenate(t_rows, axis=1)
        same = wider
        b *= 2
    return t


def _rwkv_kernel(r_ref, lw_ref, k_ref, v_ref, kk_ref, a_ref, g_ref, rk_ref, lnw_ref, lnb_ref,
                 o_ref, z_s):
    @pl.when(pl.program_id(1) == 0)
    def _():
        z_s[...] = jnp.zeros_like(z_s)

    for _ in _rwkv_pairs(r_ref, lw_ref, k_ref, v_ref, kk_ref, a_ref, g_ref, rk_ref, lnw_ref, lnb_ref,
                         o_ref, z_s, 0, r_ref.shape[1]):
        pass


def _rwkv_pairs(r_ref, lw_ref, k_ref, v_ref, kk_ref, a_ref, g_ref, rk_ref, lnw_ref, lnb_ref,
                o_ref, z_s, p0, p1):
    _, _, CB, L, _ = r_ref.shape
    NP = p1 - p0
    NB = NP * CB

    def chunks(ref):
        return ref[0, p0:p1].astype(F32).reshape(NB, L, LANES)

    r, lw, k, v, kk, a = (chunks(t) for t in (r_ref, lw_ref, k_ref, v_ref, kk_ref, a_ref))

    m0 = _head_mask((1, L, LANES))

    zero = jnp.zeros((), BF16)

    def stack(t):
        tb = t.astype(BF16)
        return jnp.concatenate([jnp.where(m0, tb, zero), jnp.where(m0, zero, tb)], axis=1)

    def unstack(t):
        return t[:, :L] + t[:, L:]

    ti = lax.broadcasted_iota(jnp.int32, (L, L), 0)
    si = lax.broadcasted_iota(jnp.int32, (L, L), 1)
    tri = jnp.broadcast_to(jnp.where(ti >= si, 1.0, 0.0).astype(BF16), (NB, L, L))
    t2 = lax.broadcasted_iota(jnp.int32, (2 * L, 2 * L), 0) % L
    s2 = lax.broadcasted_iota(jnp.int32, (2 * L, 2 * L), 1) % L
    strict = t2 > s2
    incl = t2 >= s2
    ri = lax.broadcasted_iota(jnp.int32, (LANES, LANES), 0)
    ci = lax.broadcasted_iota(jnp.int32, (LANES, LANES), 1)
    diag = ri == ci

    hi, lo = _split2(lw)
    cs = _bmm(jnp.concatenate([tri, tri], axis=2), jnp.concatenate([hi, lo], axis=1))
    yield
    tot = cs[:, L - 1:L, :]
    p_tot = jnp.exp(tot)
    kka = kk * a
    a2 = stack(-(kk * jnp.exp(cs - lw)))
    r_dec = r * jnp.exp(cs)
    r2 = stack(r_dec)
    p_inv = jnp.exp(-cs)
    b2 = stack(kka * p_inv)
    k2 = stack(k * p_inv)
    yield
    p_rest = jnp.exp(tot - cs)
    bh2 = stack(kka * p_rest)
    kh2 = stack(k * p_rest)
    v2 = stack(v)

    ar = jnp.concatenate([a2, r2], axis=1)
    bk = jnp.concatenate([b2, k2], axis=1)
    m = _bmm_nt(ar, bk)
    yield
    H = 2 * L
    n_ab = jnp.where(strict, m[:, :H, :H], 0.0)
    a_ak = jnp.where(strict, m[:, :H, H:], 0.0).astype(BF16)
    a_rb = jnp.where(incl, m[:, H:, :H], 0.0).astype(BF16)
    a_rk = jnp.where(incl, m[:, H:, H:], 0.0).astype(BF16)
    akv = _bmm(a_ak, v2)
    yield
    t_inv = (yield from _tri_inverse(n_ab, L)).astype(BF16)
    x = jnp.concatenate([a2, akv.astype(BF16)], axis=2)
    wu = _bmm(t_inv, x).astype(BF16)
    yield
    wuv = jnp.concatenate([wu, jnp.concatenate([jnp.zeros_like(v2), v2], axis=2)], axis=1)
    rbw = _bmm(jnp.concatenate([a_rb, a_rk], axis=2), wuv)
    yield
    rw = (r_dec + unstack(rbw[:, :, :LANES])).astype(BF16)
    yv = unstack(rbw[:, :, LANES:])
    gw = _bmm_tn(jnp.concatenate([bh2, kh2], axis=1), wuv)
    yield
    g_mat = gw[:, :, :LANES] + jnp.where(diag, p_tot, 0.0)
    h_mat = gw[:, :, LANES:]

    rw = rw.reshape(NP, CB, L, LANES)
    yv = yv.reshape(NP, CB, L, LANES)
    g_hi, g_lo = _split2(g_mat)
    g_hl = jnp.concatenate([g_hi, g_lo], axis=2).reshape(NP, CB, LANES, 2 * LANES)
    h_mat = h_mat.reshape(NP, CB, LANES, LANES)
    z = z_s[p0:p1]
    ys = []
    for c in range(CB):
        z_hi, z_lo = _split2(z)
        ys.append(_bmm(rw[:, c], z_hi) + yv[:, c])
        zz = jnp.concatenate([jnp.concatenate([z_hi, z_lo], axis=2),
                              jnp.concatenate([z_hi, jnp.zeros_like(z_lo)], axis=2)], axis=1)
        gz = _bmm(g_hl[:, c], zz)
        z = (gz[:, :, :LANES] + gz[:, :, LANES:]) + h_mat[:, c]
        yield
    z_s[p0:p1] = z
    y = jnp.stack(ys, axis=1).reshape(NB * L, LANES)
    hm = _head_mask((NB * L, LANES))

    def head_sum(t):
        s0 = jnp.sum(jnp.where(hm, t, 0.0), axis=-1, keepdims=True)
        s1 = jnp.sum(jnp.where(hm, 0.0, t), axis=-1, keepdims=True)
        return jnp.where(hm, s0, s1)

    def rows(t):
        return t.reshape(NP, CB * L, LANES)

    inv_n = 1.0 / HEAD_DIM
    yc = y - head_sum(y) * inv_n
    yield
    var = head_sum(yc * yc) * inv_n
    yn = rows(yc * lax.rsqrt(var + GN_EPS)) * lnw_ref[p0:p1] + lnb_ref[p0:p1]
    yield
    rk = rows(r * k) * rk_ref[p0:p1]
    bonus = rows(head_sum(rk.reshape(NB * L, LANES))) * rows(v)
    gate = g_ref[0, p0:p1].astype(F32).reshape(NP, CB * L, LANES)
    o_ref[0, p0:p1] = ((yn + bonus) * gate).astype(BF16).reshape(NP, CB, L, LANES)


def _rwkv_mix(r, lw, k, v, kk, a, gate, r_k, ln_w, ln_b):
    B, NP, S, _ = r.shape
    L, CB = RWKV_CHUNK, RWKV_CHUNKS_PER_STEP
    nc = S // L
    chunked = lambda t: t.reshape(B, NP, nc, L, LANES)
    blk = pl.BlockSpec((1, NP, CB, L, LANES), lambda b, c: (b, 0, c, 0, 0))
    par = _full_spec((NP, 1, LANES))
    pv = lambda t: t.reshape(NP, 1, LANES)
    out = pl.pallas_call(
        _rwkv_kernel,
        grid=(B, nc // CB),
        in_specs=[blk] * 7 + [par] * 3,
        out_specs=blk,
        out_shape=jax.ShapeDtypeStruct((B, NP, nc, L, LANES), BF16),
        scratch_shapes=[pltpu.VMEM((NP, LANES, LANES), F32)],
        compiler_params=_cparams("parallel", "arbitrary"),
    )(*(chunked(t) for t in (r, lw, k, v, kk, a, gate)), pv(r_k), pv(ln_w), pv(ln_b))
    return out.reshape(B, NP, S, LANES)


def kernel(x, norm_mix, norm_mlp, norm_final, attn_w_in, attn_w_out, diff_lambda, diff_subln, rwkv_mu, rwkv_w_r, rwkv_w_k, rwkv_w_v, rwkv_w_o, rwkv_w0, rwkv_w1, rwkv_w2, rwkv_a0, rwkv_a1, rwkv_a2, rwkv_g1, rwkv_g2, rwkv_k_k, rwkv_k_a, rwkv_r_k, rwkv_ln_w, rwkv_ln_b, mlp_w1, mlp_w2):
    lam_init = 0.8 - 0.6 * math.exp(-0.3 * 0)
    qkv, vt = _qkv_proj(x, norm_mix[0], attn_w_in[0])
    oa = _attn_a(qkv)
    ob = _attn_b(qkv, vt, diff_lambda[0], diff_subln[0], lam_init)
    h = _mix_mlp(x, [oa, ob], attn_w_out[0], norm_mlp[0], mlp_w1[0], mlp_w2[0])
    r, lw, k, v, kk, a, gate = _rwkv_proj(
        h, norm_mix[1], rwkv_mu[0], rwkv_w_r[0], rwkv_w_k[0], rwkv_w_v[0], rwkv_w1[0], rwkv_w2[0],
        rwkv_a1[0], rwkv_a2[0], rwkv_g1[0], rwkv_g2[0], rwkv_w0[0], rwkv_a0[0], rwkv_k_k[0], rwkv_k_a[0])
    mix = _rwkv_mix(r, lw, k, v, kk, a, gate, rwkv_r_k[0], rwkv_ln_w[0], rwkv_ln_b[0])
    return _mix_mlp(h, [mix], rwkv_w_o[0], norm_mlp[1], mlp_w1[1], mlp_w2[1], g_final=norm_final)
```
